```python
import jax, jax.numpy as jnp
from jax import lax
import numpy as np

D_MODEL = 1024
BATCH = 16
SEQ = 2048
DEPTH = 2

CHUNK = 64
EPS = 1e-6

A_HEADS = 8
A_WIDTH = D_MODEL // 2
A_HEAD_DIM = A_WIDTH // A_HEADS
A_LEFT_CHUNKS = 8
A_BAND = (A_LEFT_CHUNKS + 1) * CHUNK
REL_CLIP = 2 * CHUNK

B_HEADS = 4
B_V_WIDTH = D_MODEL // 2
B_QK_WIDTH = B_V_WIDTH // 2
B_KEY_DIM = B_QK_WIDTH // B_HEADS
B_VAL_DIM = B_V_WIDTH // B_HEADS
B_GATE_RANK = 16
B_GATE_TEMP = 16.0

N_BRANCH = 2
IN_SPLIT = (A_WIDTH, A_WIDTH, A_WIDTH,
            B_QK_WIDTH, B_QK_WIDTH, B_V_WIDTH,
            B_V_WIDTH, B_GATE_RANK,
            N_BRANCH * D_MODEL)
IN_COLS = sum(IN_SPLIT)
SPLIT_POINTS = tuple(sum(IN_SPLIT[:i + 1]) for i in range(len(IN_SPLIT) - 1))

N_GROUPS = 4
EXPERTS_PER_GROUP = 8
N_EXPERTS = N_GROUPS * EXPERTS_PER_GROUP
TOP_K = 2
D_EXPERT = D_MODEL // 4

kernel_name = "hybrid_chunkattn_gla_hiermoe"


def rms_norm(x, gain):
    xf = x.astype(jnp.float32)
    y = xf * lax.rsqrt(jnp.mean(xf * xf, axis=-1, keepdims=True) + EPS)
    return (y * gain.astype(jnp.float32)).astype(x.dtype)


def chunk_band_attention(q, k, v, q_gain, k_gain, rel_bias):
    b_, s_, h_, dh = q.shape
    nc = s_ // CHUNK
    pad = A_LEFT_CHUNKS * CHUNK
    q = rms_norm(q, q_gain) * (dh ** -0.5)
    k = rms_norm(k, k_gain)
    k_pad = jnp.pad(k, ((0, 0), (pad, 0), (0, 0), (0, 0)))
    v_pad = jnp.pad(v, ((0, 0), (pad, 0), (0, 0), (0, 0)))
    q_pos = jnp.arange(CHUNK)[:, None] + pad
    k_pos = jnp.arange(A_BAND)[None, :]
    rel = jnp.clip(q_pos - k_pos, -REL_CLIP, REL_CLIP) + REL_CLIP
    bias = jnp.transpose(rel_bias[rel], (2, 0, 1)).astype(jnp.float32)
    q_chunks = jnp.transpose(q.reshape(b_, nc, CHUNK, h_, dh), (1, 0, 2, 3, 4))

    def one_chunk(args):
        c, q_c = args
        start = c * CHUNK
        k_band = lax.dynamic_slice_in_dim(k_pad, start, A_BAND, axis=1)
        v_band = lax.dynamic_slice_in_dim(v_pad, start, A_BAND, axis=1)
        s = jnp.einsum('bqhd,bkhd->bhqk', q_c, k_band).astype(jnp.float32) + bias
        valid = (start + jnp.arange(A_BAND)) >= pad
        s = jnp.where(valid[None, None, None, :], s, -jnp.inf)
        p = jax.nn.softmax(s, axis=-1).astype(v.dtype)
        return jnp.einsum('bhqk,bkhd->bqhd', p, v_band)

    out = lax.map(one_chunk, (jnp.arange(nc), q_chunks))
    return jnp.transpose(out, (1, 0, 2, 3, 4)).reshape(b_, s_, h_ * dh)


def gla_chunked(q, k, v, log_a):
    b_, s_, h_, dk = q.shape
    dv = v.shape[-1]
    nc = s_ // CHUNK
    out_dtype = v.dtype

    def to_chunks(t):
        t = t.astype(jnp.float32).reshape(b_, nc, CHUNK, h_, t.shape[-1])
        return jnp.transpose(t, (1, 0, 3, 2, 4))

    qc = to_chunks(q) * (dk ** -0.5)
    kc, vc, gc = to_chunks(k), to_chunks(v), to_chunks(log_a)
    causal = jnp.tril(jnp.ones((CHUNK, CHUNK), dtype=bool))

    def step(state, inp):
        q_c, k_c, v_c, g_c = inp
        b = jnp.cumsum(g_c, axis=2)
        o_inter = jnp.einsum('bhck,bhkv->bhcv', q_c * jnp.exp(b), state)
        diff = b[:, :, :, None, :] - b[:, :, None, :, :]
        decay = jnp.exp(jnp.where(causal[:, :, None], diff, -jnp.inf))
        attn = jnp.einsum('bhik,bhjk,bhijk->bhij', q_c, k_c, decay)
        o_intra = jnp.einsum('bhij,bhjv->bhiv', attn, v_c)
        b_last = b[:, :, -1:, :]
        new_state = (jnp.exp(b_last[:, :, 0, :])[..., None] * state
                     + jnp.einsum('bhck,bhcv->bhkv', k_c * jnp.exp(b_last - b), v_c))
        return new_state, o_inter + o_intra

    state0 = jnp.zeros((b_, h_, dk, dv), jnp.float32)
    _, o = lax.scan(step, state0, (qc, kc, vc, gc))
    o = jnp.transpose(o, (1, 0, 3, 2, 4)).reshape(b_, s_, h_, dv)
    return o.astype(out_dtype)


def hier_moe(h, w_group, b_group, w_router, b_router, w_gate, w_up, w_down):
    shape = h.shape
    hf = h.reshape(-1, shape[-1])
    t_ = hf.shape[0]
    g_prob = jax.nn.softmax((hf @ w_group).astype(jnp.float32) + b_group, axis=-1)
    g_top, g_idx = lax.top_k(g_prob, 1)
    e_logits = ((hf @ w_router).astype(jnp.float32) + b_router).reshape(t_, N_GROUPS, EXPERTS_PER_GROUP)
    e_in_group = jnp.take_along_axis(e_logits, g_idx[:, :, None], axis=1)[:, 0]
    e_top, e_idx = lax.top_k(e_in_group, TOP_K)
    e_w = jax.nn.softmax(e_top, axis=-1) * g_top
    expert_id = g_idx * EXPERTS_PER_GROUP + e_idx
    gates = jnp.sum(jax.nn.one_hot(expert_id, N_EXPERTS, dtype=jnp.float32) * e_w[..., None], axis=1)
    gates = gates.astype(h.dtype)
    y = jnp.zeros_like(hf)
    for e in range(N_EXPERTS):
        a = jax.nn.silu(hf @ w_gate[e]) * (hf @ w_up[e])
        y = y + gates[:, e:e + 1] * (a @ w_down[e])
    return y.reshape(shape)


def setup_inputs(seed: int = 0) -> dict:
    key = jax.random.key(seed)
    ks = jax.random.split(key, 20)
    f32 = jnp.float32

    def nrm(k, shape, scale):
        return jax.random.normal(k, shape, f32) * scale

    return {
        "x": nrm(ks[0], (BATCH, SEQ, D_MODEL), 1.0),
        "norm_mix_gain": 1.0 + nrm(ks[1], (DEPTH, D_MODEL), 0.02),
        "w_in": nrm(ks[2], (DEPTH, D_MODEL, IN_COLS), D_MODEL ** -0.5),
        "a_q_gain": 1.0 + nrm(ks[3], (DEPTH, A_HEADS, A_HEAD_DIM), 0.02),
        "a_k_gain": 1.0 + nrm(ks[4], (DEPTH, A_HEADS, A_HEAD_DIM), 0.02),
        "rel_bias": nrm(ks[5], (2 * REL_CLIP + 1, A_HEADS), 0.2),
        "b_gate_up": nrm(ks[6], (DEPTH, B_GATE_RANK, B_QK_WIDTH), B_GATE_RANK ** -0.5),
        "b_gate_bias": nrm(ks[7], (DEPTH, B_QK_WIDTH), 0.1),
        "b_out_gain": 1.0 + nrm(ks[8], (DEPTH, B_HEADS, B_VAL_DIM), 0.02),
        "proj_a": nrm(ks[9], (DEPTH, A_WIDTH, D_MODEL), A_WIDTH ** -0.5),
        "proj_b": nrm(ks[10], (DEPTH, B_V_WIDTH, D_MODEL), B_V_WIDTH ** -0.5),
        "w_out": nrm(ks[11], (DEPTH, D_MODEL, D_MODEL), D_MODEL ** -0.5),
        "norm_ffn_gain": 1.0 + nrm(ks[12], (DEPTH, D_MODEL), 0.02),
        "w_group": nrm(ks[13], (DEPTH, D_MODEL, N_GROUPS), D_MODEL ** -0.5),
        "b_group": nrm(ks[14], (DEPTH, N_GROUPS), 0.01),
        "w_router": nrm(ks[15], (DEPTH, D_MODEL, N_EXPERTS), D_MODEL ** -0.5),
        "b_router": nrm(ks[16], (DEPTH, N_EXPERTS), 0.01),
        "w_gate": nrm(ks[17], (DEPTH, N_EXPERTS, D_MODEL, D_EXPERT), D_MODEL ** -0.5),
        "w_up": nrm(ks[18], (DEPTH, N_EXPERTS, D_MODEL, D_EXPERT), D_MODEL ** -0.5),
        "w_down": nrm(ks[19], (DEPTH, N_EXPERTS, D_EXPERT, D_MODEL), D_EXPERT ** -0.5),
    }


def reference(x, norm_mix_gain, w_in, a_q_gain, a_k_gain, rel_bias, b_gate_up, b_gate_bias,
              b_out_gain, proj_a, proj_b, w_out, norm_ffn_gain, w_group, b_group, w_router,
              b_router, w_gate, w_up, w_down):
    b_, s_, _ = x.shape
    for l in range(DEPTH):
        h = rms_norm(x, norm_mix_gain[l])
        u = h @ w_in[l]
        a_q, a_k, a_v, b_q, b_k, b_v, b_r, b_lr, gate_cols = jnp.split(u, SPLIT_POINTS, axis=-1)

        y_a = chunk_band_attention(a_q.reshape(b_, s_, A_HEADS, A_HEAD_DIM),
                                   a_k.reshape(b_, s_, A_HEADS, A_HEAD_DIM),
                                   a_v.reshape(b_, s_, A_HEADS, A_HEAD_DIM),
                                   a_q_gain[l], a_k_gain[l], rel_bias)

        log_a = jax.nn.log_sigmoid((b_lr @ b_gate_up[l]).astype(jnp.float32)
                                   + b_gate_bias[l].astype(jnp.float32)) / B_GATE_TEMP
        o_b = gla_chunked(b_q.reshape(b_, s_, B_HEADS, B_KEY_DIM),
                          b_k.reshape(b_, s_, B_HEADS, B_KEY_DIM),
                          b_v.reshape(b_, s_, B_HEADS, B_VAL_DIM),
                          log_a.reshape(b_, s_, B_HEADS, B_KEY_DIM))
        y_b = rms_norm(o_b, b_out_gain[l]).reshape(b_, s_, B_V_WIDTH) * jax.nn.silu(b_r)

        g_a, g_b = jnp.split(jax.nn.sigmoid(gate_cols), N_BRANCH, axis=-1)
        merged = g_a * (y_a @ proj_a[l]) + g_b * (y_b @ proj_b[l])
        x = x + merged @ w_out[l]

        h = rms_norm(x, norm_ffn_gain[l])
        x = x + hier_moe(h, w_group[l], b_group[l], w_router[l], b_router[l],
                         w_gate[l], w_up[l], w_down[l])
    return x
```

```python
import functools

import jax
import jax.numpy as jnp
from jax import lax
from jax.experimental import pallas as pl
from jax.experimental.pallas import tpu as pltpu

F32 = jnp.float32
BF16 = jnp.bfloat16
HIGHEST = lax.Precision.HIGHEST

EPS = 1e-6
CHUNK = 64
A_HEADS = 8
A_LEFT_CHUNKS = 8
REL_CLIP = 2 * CHUNK
B_HEADS = 4
B_GATE_RANK = 16
B_GATE_TEMP = 16.0
N_GROUPS = 4
EXPERTS_PER_GROUP = 8
N_EXPERTS = N_GROUPS * EXPERTS_PER_GROUP

LANES = 128
MASK_NEG = -1e30
VMEM_LIMIT = 56 * 1024 * 1024

ATT_Q = 256
ATT_K = ATT_Q + A_LEFT_CHUNKS * CHUNK
GLA_ROWS = 256
SUB = 16


def _params(sem):
    return pltpu.CompilerParams(dimension_semantics=sem, vmem_limit_bytes=VMEM_LIMIT)


def _const_spec(shape):
    nd = len(shape)
    return pl.BlockSpec(shape, lambda *_: (0,) * nd, pipeline_mode=pl.Buffered(1))


def _dot(a, b):
    return jnp.dot(a, b, preferred_element_type=F32)


def _dot_nt(a, b):
    return lax.dot_general(a, b, (((1,), (1,)), ((), ())), preferred_element_type=F32)


def _dot_tn(a, b):
    return lax.dot_general(a, b, (((0,), (0,)), ((), ())), preferred_element_type=F32)


def _rms(x, gain):
    return x * lax.rsqrt(jnp.mean(x * x, axis=-1, keepdims=True) + EPS) * gain


def _head_norm(acc, ones_bd, gain, head_dim, scale):
    sq = acc * acc
    hi = sq.astype(BF16)
    lo = (sq - hi.astype(F32)).astype(BF16)
    ss = _dot(hi, ones_bd) + _dot(lo, ones_bd)
    inv = lax.rsqrt(ss * (1.0 / head_dim) + EPS)
    return acc * inv * (gain * scale)


def _inproj_kernel(x_ref, gain_ref, w_ref, wlr_ref, qg_ref, kg_ref, ones_ref,
                   u_ref, lr_ref, *, a_width, head_dim, col_chunk):
    x = x_ref[...]
    h = _rms(x, gain_ref[...]).astype(BF16)
    n_cols = w_ref.shape[1]
    for c0 in range(0, n_cols, col_chunk):
        acc = _dot(h, w_ref[:, c0:c0 + col_chunk])
        if c0 == 0:
            acc = _head_norm(acc, ones_ref[...], qg_ref[...], head_dim, head_dim ** -0.5)
        elif c0 == a_width:
            acc = _head_norm(acc, ones_ref[...], kg_ref[...], head_dim, 1.0)
        u_ref[:, c0:c0 + col_chunk] = acc.astype(BF16)
    lr_ref[...] = _dot(h, wlr_ref[...])


def _inproj(x2d, gain, w_main, w_lr, q_gain, k_gain, ones_bd, *, tm, a_width, head_dim):
    t, d = x2d.shape
    n = w_main.shape[1]
    kern = functools.partial(_inproj_kernel, a_width=a_width, head_dim=head_dim,
                             col_chunk=a_width)
    return pl.pallas_call(
        kern,
        grid=(t // tm,),
        in_specs=[
            pl.BlockSpec((tm, d), lambda i: (i, 0)),
            _const_spec((1, d)),
            _const_spec((d, n)),
            _const_spec((d, LANES)),
            _const_spec((1, a_width)),
            _const_spec((1, a_width)),
            _const_spec((a_width, a_width)),
        ],
        out_specs=[
            pl.BlockSpec((tm, n), lambda i: (i, 0)),
            pl.BlockSpec((tm, LANES), lambda i: (i, 0)),
        ],
        out_shape=[
            jax.ShapeDtypeStruct((t, n), BF16),
            jax.ShapeDtypeStruct((t, LANES), F32),
        ],
        compiler_params=_params(("parallel",)),
        name="inproj",
    )(x2d, gain, w_main, w_lr, q_gain, k_gain, ones_bd)


def _attn_kernel(q_ref, k0_ref, k1_ref, k2_ref, v0_ref, v1_ref, v2_ref, bias_ref, o_ref,
                 *, heads, head_dim):
    i = pl.program_id(1)
    pen = (jnp.where(i >= 2, 0.0, MASK_NEG), jnp.where(i >= 1, 0.0, MASK_NEG), None)
    k_refs = (k0_ref, k1_ref, k2_ref)
    v_refs = (v0_ref, v1_ref, v2_ref)
    nq = q_ref.shape[0]
    for h in range(heads):
        hs = slice(h * head_dim, (h + 1) * head_dim)
        q = q_ref[:, hs]
        s = []
        for j in range(3):
            sj = _dot_nt(q, k_refs[j][:, hs]) + bias_ref[h, :, j * nq:(j + 1) * nq]
            if pen[j] is not None:
                sj = sj + pen[j]
            s.append(sj)
        m = jnp.maximum(jnp.maximum(jnp.max(s[0], axis=-1, keepdims=True),
                                    jnp.max(s[1], axis=-1, keepdims=True)),
                        jnp.max(s[2], axis=-1, keepdims=True))
        p = [jnp.exp(sj - m) for sj in s]
        l = (jnp.sum(p[0], axis=-1, keepdims=True) + jnp.sum(p[1], axis=-1, keepdims=True)
             + jnp.sum(p[2], axis=-1, keepdims=True))
        o = (_dot(p[0].astype(BF16), v_refs[0][:, hs]) + _dot(p[1].astype(BF16), v_refs[1][:, hs])
             + _dot(p[2].astype(BF16), v_refs[2][:, hs]))
        o_ref[:, hs] = (o / l).astype(BF16)


def _attention(u3, bias_tile, *, heads, head_dim):
    b, s, _ = u3.shape
    width = heads * head_dim
    nblk = s // ATT_Q
    kern = functools.partial(_attn_kernel, heads=heads, head_dim=head_dim)

    def kv_spec(col, back):
        return pl.BlockSpec((None, ATT_Q, width),
                            lambda bi, i: (bi, jnp.maximum(i - back, 0), col))

    return pl.pallas_call(
        kern,
        grid=(b, nblk),
        in_specs=[
            pl.BlockSpec((None, ATT_Q, width), lambda bi, i: (bi, i, 0)),
            kv_spec(1, 2), kv_spec(1, 1), kv_spec(1, 0),
            kv_spec(2, 2), kv_spec(2, 1), kv_spec(2, 0),
            _const_spec(bias_tile.shape),
        ],
        out_specs=pl.BlockSpec((None, ATT_Q, width), lambda bi, i: (bi, i, 0)),
        out_shape=jax.ShapeDtypeStruct((b, s, width), BF16),
        compiler_params=_params(("parallel", "parallel")),
        name="band_attn",
    )(u3, u3, u3, u3, u3, u3, u3, bias_tile)


def _attn_bias_tile(rel_bias):
    pad = A_LEFT_CHUNKS * CHUNK
    r = jnp.arange(ATT_Q)[:, None]
    c = jnp.arange(ATT_K)[None, :]
    rel = jnp.clip(r + pad - c, -REL_CLIP, REL_CLIP) + REL_CLIP
    bias = jnp.transpose(rel_bias[rel], (2, 0, 1)).astype(F32)
    qc, kc = r // CHUNK, c // CHUNK
    allowed = (kc >= qc) & (kc <= qc + A_LEFT_CHUNKS)
    return jnp.where(allowed[None], bias, MASK_NEG)


def _log_sigmoid(z):
    return jnp.minimum(z, 0.0) - jnp.log1p(jnp.exp(-jnp.abs(z)))


def _gla_kernel(q_ref, k_ref, v_ref, r_ref, lr_ref, gup_ref, gbias_ref, ogain_ref, tri_ref,
                y_ref, state_ref, *, heads, dk, dv):
    @pl.when(pl.program_id(1) == 0)
    def _():
        state_ref[...] = jnp.zeros_like(state_ref)

    rows = q_ref.shape[0]
    qk_w = heads * dk
    row_id = lax.broadcasted_iota(jnp.int32, (CHUNK, qk_w), 0)
    ci = lax.broadcasted_iota(jnp.int32, (CHUNK, CHUNK), 0)
    cj = lax.broadcasted_iota(jnp.int32, (CHUNK, CHUNK), 1)
    causal = ci >= cj
    n_sub = CHUNK // SUB

    for c in range(rows // CHUNK):
        rs = slice(c * CHUNK, (c + 1) * CHUNK)
        z = jnp.dot(lr_ref[rs, :], gup_ref[...], precision=HIGHEST,
                    preferred_element_type=F32) + gbias_ref[...]
        log_a = _log_sigmoid(z) * (1.0 / B_GATE_TEMP)
        b = jnp.dot(tri_ref[...], log_a, precision=HIGHEST, preferred_element_type=F32)
        q = q_ref[rs, :].astype(F32) * (dk ** -0.5)
        k = k_ref[rs, :].astype(F32)
        b_last = b[CHUNK - 1:CHUNK, :]

        ref_rows = [b[(j + 1) * SUB - 1:(j + 1) * SUB, :] for j in range(n_sub)]
        ref_full = jnp.concatenate(
            [jnp.broadcast_to(rj, (SUB, qk_w)) for rj in ref_rows], axis=0)
        k_dec = k * jnp.exp(ref_full - b)
        q_dec = [q * jnp.exp(jnp.where(row_id >= j * SUB, b - ref_rows[j], -jnp.inf))
                 for j in range(n_sub)]
        k_sub = [jnp.where((row_id >= j * SUB) & (row_id < (j + 1) * SUB), k_dec, 0.0)
                 for j in range(n_sub)]
        q_in = (q * jnp.exp(b)).astype(BF16)
        k_out = (k * jnp.exp(b_last - b)).astype(BF16)
        e_last = jnp.exp(b_last)

        for h in range(heads):
            ks = slice(h * dk, (h + 1) * dk)
            vs = slice(h * dv, (h + 1) * dv)
            q_cat = jnp.concatenate([qd[:, ks] for qd in q_dec], axis=1).astype(BF16)
            k_cat = jnp.concatenate([kz[:, ks] for kz in k_sub], axis=1).astype(BF16)
            attn = jnp.where(causal, _dot_nt(q_cat, k_cat), 0.0)
            v_h = v_ref[rs, vs]
            st = state_ref[h]
            o = _dot(attn.astype(BF16), v_h) + _dot_nt(q_in[:, ks], st.astype(BF16))
            state_ref[h] = st * e_last[:, ks] + _dot_tn(v_h, k_out[:, ks])
            r_h = r_ref[rs, vs].astype(F32)
            y = _rms(o, ogain_ref[:, vs]) * (r_h * jax.nn.sigmoid(r_h))
            y_ref[rs, vs] = y.astype(BF16)


def _gla(u3, lr3, gup_pad, gbias, ogain, tri, *, heads, dk, dv, qk_col, v_col, r_col):
    b, s, _ = u3.shape
    qk_w, v_w = heads * dk, heads * dv
    kern = functools.partial(_gla_kernel, heads=heads, dk=dk, dv=dv)
    return pl.pallas_call(
        kern,
        grid=(b, s // GLA_ROWS),
        in_specs=[
            pl.BlockSpec((None, GLA_ROWS, qk_w), lambda bi, i: (bi, i, qk_col)),
            pl.BlockSpec((None, GLA_ROWS, qk_w), lambda bi, i: (bi, i, qk_col + 1)),
            pl.BlockSpec((None, GLA_ROWS, v_w), lambda bi, i: (bi, i, v_col)),
            pl.BlockSpec((None, GLA_ROWS, v_w), lambda bi, i: (bi, i, r_col)),
            pl.BlockSpec((None, GLA_ROWS, LANES), lambda bi, i: (bi, i, 0)),
            _const_spec(gup_pad.shape),
            _const_spec(gbias.shape),
            _const_spec(ogain.shape),
            _const_spec(tri.shape),
        ],
        out_specs=pl.BlockSpec((None, GLA_ROWS, v_w), lambda bi, i: (bi, i, 0)),
        out_shape=jax.ShapeDtypeStruct((b, s, v_w), BF16),
        scratch_shapes=[pltpu.VMEM((heads, dv, dk), F32)],
        compiler_params=_params(("parallel", "arbitrary")),
        name="gla",
    )(u3, u3, u3, u3, lr3, gup_pad, gbias, ogain, tri)


def _route(logits):
    lane = lax.broadcasted_iota(jnp.int32, logits.shape, 1)
    neg_inf = -jnp.inf
    gl = jnp.where(lane < N_GROUPS, logits, neg_inf)
    gmax = jnp.max(gl, axis=-1, keepdims=True)
    g_idx = jnp.min(jnp.where(gl == gmax, lane, LANES), axis=-1, keepdims=True)
    g_top = 1.0 / jnp.sum(jnp.exp(gl - gmax), axis=-1, keepdims=True)
    e_lane = lane - N_GROUPS
    in_group = (e_lane >= g_idx * EXPERTS_PER_GROUP) & (e_lane < (g_idx + 1) * EXPERTS_PER_GROUP)
    el = jnp.where(in_group, logits, neg_inf)
    e1 = jnp.max(el, axis=-1, keepdims=True)
    i1 = jnp.min(jnp.where(el == e1, lane, LANES), axis=-1, keepdims=True)
    el2 = jnp.where(lane == i1, neg_inf, el)
    e2 = jnp.max(el2, axis=-1, keepdims=True)
    i2 = jnp.min(jnp.where(el2 == e2, lane, LANES), axis=-1, keepdims=True)
    t = jnp.exp(e2 - e1)
    w1 = g_top / (1.0 + t)
    w2 = g_top * t / (1.0 + t)
    gates = jnp.where(lane == i1, w1, 0.0) + jnp.where(lane == i2, w2, 0.0)
    return pltpu.roll(gates, LANES - N_GROUPS, axis=1)


def _post_kernel(x_ref, ya_ref, yb_ref, ga_ref, gb_ref, pa_ref, pb_ref, wo_ref, gain_ref,
                 wr_ref, br_ref, xmid_ref, h_ref, gates_ref):
    ga = jax.nn.sigmoid(ga_ref[...].astype(F32))
    gb = jax.nn.sigmoid(gb_ref[...].astype(F32))
    merged = ga * _dot(ya_ref[...], pa_ref[...]) + gb * _dot(yb_ref[...], pb_ref[...])
    x_mid = x_ref[...] + _dot(merged.astype(BF16), wo_ref[...])
    xmid_ref[...] = x_mid
    h = _rms(x_mid, gain_ref[...])
    h_ref[...] = h.astype(BF16)
    logits = jnp.dot(h, wr_ref[...], precision=HIGHEST, preferred_element_type=F32) + br_ref[...]
    gates_ref[...] = _route(logits)


def _post(x2d, ya, yb, u2d, pa, pb, wo, gain, wr, br, *, tm, ga_col, gb_col):
    t, d = x2d.shape
    aw, bw = ya.shape[1], yb.shape[1]
    return pl.pallas_call(
        _post_kernel,
        grid=(t // tm,),
        in_specs=[
            pl.BlockSpec((tm, d), lambda i: (i, 0)),
            pl.BlockSpec((tm, aw), lambda i: (i, 0)),
            pl.BlockSpec((tm, bw), lambda i: (i, 0)),
            pl.BlockSpec((tm, d), lambda i: (i, ga_col)),
            pl.BlockSpec((tm, d), lambda i: (i, gb_col)),
            _const_spec(pa.shape), _const_spec(pb.shape), _const_spec(wo.shape),
            _const_spec(gain.shape), _const_spec(wr.shape), _const_spec(br.shape),
        ],
        out_specs=[
            pl.BlockSpec((tm, d), lambda i: (i, 0)),
            pl.BlockSpec((tm, d), lambda i: (i, 0)),
            pl.BlockSpec((tm, LANES), lambda i: (i, 0)),
        ],
        out_shape=[
            jax.ShapeDtypeStruct((t, d), F32),
            jax.ShapeDtypeStruct((t, d), BF16),
            jax.ShapeDtypeStruct((t, LANES), F32),
        ],
        compiler_params=_params(("parallel",)),
        name="merge_route",
    )(x2d, ya, yb, u2d, u2d, pa, pb, wo, gain, wr, br)


def _moe_dense_kernel(x_ref, h_ref, gates_ref, wg_ref, wu_ref, wd_ref, o_ref):
    e = pl.program_id(1)

    @pl.when(e == 0)
    def _():
        o_ref[...] = x_ref[...]

    h = h_ref[...]
    a = _dot(h, wg_ref[...])
    a = a * jax.nn.sigmoid(a) * _dot(h, wu_ref[...])
    y = _dot(a.astype(BF16), wd_ref[...])
    gates = gates_ref[...]
    lane = lax.broadcasted_iota(jnp.int32, gates.shape, 1)
    g = jnp.sum(jnp.where(lane == e, gates, 0.0), axis=-1, keepdims=True)
    o_ref[...] += g * y


def _moe_dense(xmid, h, gates, wg, wu, wd, *, tm):
    t, d = xmid.shape
    ne, _, de = wg.shape
    return pl.pallas_call(
        _moe_dense_kernel,
        grid=(t // tm, ne),
        in_specs=[
            pl.BlockSpec((tm, d), lambda i, e: (i, 0)),
            pl.BlockSpec((tm, d), lambda i, e: (i, 0)),
            pl.BlockSpec((tm, LANES), lambda i, e: (i, 0)),
            pl.BlockSpec((None, d, de), lambda i, e: (e, 0, 0)),
            pl.BlockSpec((None, d, de), lambda i, e: (e, 0, 0)),
            pl.BlockSpec((None, de, d), lambda i, e: (e, 0, 0)),
        ],
        out_specs=pl.BlockSpec((tm, d), lambda i, e: (i, 0)),
        out_shape=jax.ShapeDtypeStruct((t, d), F32),
        compiler_params=_params(("parallel", "arbitrary")),
        name="moe_dense",
    )(xmid, h, gates, wg, wu, wd)


def kernel(x, norm_mix_gain, w_in, a_q_gain, a_k_gain, rel_bias, b_gate_up, b_gate_bias,
           b_out_gain, proj_a, proj_b, w_out, norm_ffn_gain, w_group, b_group, w_router,
           b_router, w_gate, w_up, w_down):
    bsz, seq, d = x.shape
    depth = w_in.shape[0]
    t = bsz * seq
    a_width = proj_a.shape[1]
    a_hd = a_width // A_HEADS
    b_vw = proj_b.shape[1]
    b_qkw = b_gate_up.shape[2]
    dk, dv = b_qkw // B_HEADS, b_vw // B_HEADS
    assert seq % ATT_Q == 0 and seq % GLA_ROWS == 0 and d == 2 * a_width == 2 * b_vw
    assert b_qkw * 2 == a_width and N_GROUPS + N_EXPERTS <= LANES

    o_lr = 3 * a_width + 2 * b_qkw + 2 * b_vw
    main_cols = jnp.concatenate([jnp.arange(0, o_lr), jnp.arange(o_lr + B_GATE_RANK, w_in.shape[2])])
    qk_col = 3 * a_width // b_qkw
    v_col = (3 * a_width + 2 * b_qkw) // b_vw
    r_col = v_col + 1
    ga_col = (3 * a_width + 2 * b_qkw + 2 * b_vw) // d
    gb_col = ga_col + 1
    assert (3 * a_width + 2 * b_qkw + 2 * b_vw) % d == 0

    head_id = jnp.arange(a_width) // a_hd
    ones_bd = (head_id[:, None] == head_id[None, :]).astype(BF16)
    tri = jnp.tril(jnp.ones((CHUNK, CHUNK), F32))
    bias_tile = _attn_bias_tile(rel_bias)

    x2d = x.reshape(t, d)
    tm = 512
    for l in range(depth):
        w_main = w_in[l][:, main_cols].astype(BF16)
        w_lr = jnp.pad(w_in[l][:, o_lr:o_lr + B_GATE_RANK],
                       ((0, 0), (0, LANES - B_GATE_RANK))).astype(BF16)
        u, lr = _inproj(x2d, norm_mix_gain[l][None, :], w_main, w_lr,
                        a_q_gain[l].reshape(1, a_width), a_k_gain[l].reshape(1, a_width),
                        ones_bd, tm=tm, a_width=a_width, head_dim=a_hd)
        u3 = u.reshape(bsz, seq, u.shape[1])
        y_a = _attention(u3, bias_tile, heads=A_HEADS, head_dim=a_hd)
        gup_pad = jnp.pad(b_gate_up[l], ((0, LANES - B_GATE_RANK), (0, 0)))
        y_b = _gla(u3, lr.reshape(bsz, seq, LANES), gup_pad, b_gate_bias[l][None, :],
                   b_out_gain[l].reshape(1, b_vw), tri, heads=B_HEADS, dk=dk, dv=dv,
                   qk_col=qk_col, v_col=v_col, r_col=r_col)
        w_r = jnp.pad(jnp.concatenate([w_group[l], w_router[l]], axis=1),
                      ((0, 0), (0, LANES - N_GROUPS - N_EXPERTS)))
        b_r = jnp.pad(jnp.concatenate([b_group[l], b_router[l]]),
                      (0, LANES - N_GROUPS - N_EXPERTS))[None, :]
        x_mid, h2, gates = _post(x2d, y_a.reshape(t, a_width), y_b.reshape(t, b_vw), u,
                                 proj_a[l].astype(BF16), proj_b[l].astype(BF16),
                                 w_out[l].astype(BF16), norm_ffn_gain[l][None, :], w_r, b_r,
                                 tm=tm, ga_col=ga_col, gb_col=gb_col)
        x2d = _moe_dense(x_mid, h2, gates, w_gate[l].astype(BF16), w_up[l].astype(BF16),
                         w_down[l].astype(BF16), tm=tm)
    return x2d.reshape(bsz, seq, d)
```

```python
import functools

import jax
import jax.numpy as jnp
from jax import lax
from jax.experimental import pallas as pl
from jax.experimental.pallas import tpu as pltpu

F32 = jnp.float32
BF16 = jnp.bfloat16
HIGHEST = lax.Precision.HIGHEST

EPS = 1e-6
CHUNK = 64
A_HEADS = 8
A_LEFT_CHUNKS = 8
REL_CLIP = 2 * CHUNK
B_HEADS = 4
B_GATE_RANK = 16
B_GATE_TEMP = 16.0
N_GROUPS = 4
EXPERTS_PER_GROUP = 8
N_EXPERTS = N_GROUPS * EXPERTS_PER_GROUP

LANES = 128
MASK_NEG = -1e30
VMEM_LIMIT = 56 * 1024 * 1024

ATT_Q = 256
ATT_K = ATT_Q + A_LEFT_CHUNKS * CHUNK
GLA_ROWS = 256
SUB = 16


def _params(sem):
    return pltpu.CompilerParams(dimension_semantics=sem, vmem_limit_bytes=VMEM_LIMIT)


def _const_spec(shape):
    nd = len(shape)
    return pl.BlockSpec(shape, lambda *_: (0,) * nd, pipeline_mode=pl.Buffered(1))


def _dot(a, b):
    return jnp.dot(a, b, preferred_element_type=F32)


def _dot_nt(a, b):
    return lax.dot_general(a, b, (((1,), (1,)), ((), ())), preferred_element_type=F32)


def _dot_tn(a, b):
    return lax.dot_general(a, b, (((0,), (0,)), ((), ())), preferred_element_type=F32)


def _rms(x, gain):
    return x * lax.rsqrt(jnp.mean(x * x, axis=-1, keepdims=True) + EPS) * gain


def _head_norm(acc, ones_bd, gain, head_dim, scale):
    sq = acc * acc
    hi = sq.astype(BF16)
    lo = (sq - hi.astype(F32)).astype(BF16)
    ss = _dot(hi, ones_bd) + _dot(lo, ones_bd)
    inv = lax.rsqrt(ss * (1.0 / head_dim) + EPS)
    return acc * inv * (gain * scale)


def _inproj_kernel(x_ref, gain_ref, w_ref, wlr_ref, qg_ref, kg_ref, ones_ref,
                   u_ref, lr_ref, *, a_width, head_dim, col_chunk):
    x = x_ref[...]
    h = _rms(x, gain_ref[...]).astype(BF16)
    n_cols = w_ref.shape[1]
    for c0 in range(0, n_cols, col_chunk):
        acc = _dot(h, w_ref[:, c0:c0 + col_chunk])
        if c0 == 0:
            acc = _head_norm(acc, ones_ref[...], qg_ref[...], head_dim, head_dim ** -0.5)
        elif c0 == a_width:
            acc = _head_norm(acc, ones_ref[...], kg_ref[...], head_dim, 1.0)
        u_ref[:, c0:c0 + col_chunk] = acc.astype(BF16)
    lr_ref[...] = _dot(h, wlr_ref[...])


def _inproj(x2d, gain, w_main, w_lr, q_gain, k_gain, ones_bd, *, tm, a_width, head_dim):
    t, d = x2d.shape
    n = w_main.shape[1]
    kern = functools.partial(_inproj_kernel, a_width=a_width, head_dim=head_dim,
                             col_chunk=a_width)
    return pl.pallas_call(
        kern,
        grid=(t // tm,),
        in_specs=[
            pl.BlockSpec((tm, d), lambda i: (i, 0)),
            _const_spec((1, d)),
            _const_spec((d, n)),
            _const_spec((d, LANES)),
            _const_spec((1, a_width)),
            _const_spec((1, a_width)),
            _const_spec((a_width, a_width)),
        ],
        out_specs=[
            pl.BlockSpec((tm, n), lambda i: (i, 0)),
            pl.BlockSpec((tm, LANES), lambda i: (i, 0)),
        ],
        out_shape=[
            jax.ShapeDtypeStruct((t, n), BF16),
            jax.ShapeDtypeStruct((t, LANES), F32),
        ],
        compiler_params=_params(("parallel",)),
        name="inproj",
    )(x2d, gain, w_main, w_lr, q_gain, k_gain, ones_bd)


def _attn_kernel(q_ref, k0_ref, k1_ref, k2_ref, v0_ref, v1_ref, v2_ref, bias_ref, o_ref,
                 *, heads, head_dim):
    i = pl.program_id(1)
    pen = (jnp.where(i >= 2, 0.0, MASK_NEG), jnp.where(i >= 1, 0.0, MASK_NEG), None)
    k_refs = (k0_ref, k1_ref, k2_ref)
    v_refs = (v0_ref, v1_ref, v2_ref)
    nq = q_ref.shape[0]
    for h in range(heads):
        hs = slice(h * head_dim, (h + 1) * head_dim)
        q = q_ref[:, hs]
        s = []
        for j in range(3):
            sj = _dot_nt(q, k_refs[j][:, hs]) + bias_ref[h, :, j * nq:(j + 1) * nq]
            if pen[j] is not None:
                sj = sj + pen[j]
            s.append(sj)
        m = jnp.maximum(jnp.maximum(jnp.max(s[0], axis=-1, keepdims=True),
                                    jnp.max(s[1], axis=-1, keepdims=True)),
                        jnp.max(s[2], axis=-1, keepdims=True))
        p = [jnp.exp(sj - m) for sj in s]
        l = (jnp.sum(p[0], axis=-1, keepdims=True) + jnp.sum(p[1], axis=-1, keepdims=True)
             + jnp.sum(p[2], axis=-1, keepdims=True))
        o = (_dot(p[0].astype(BF16), v_refs[0][:, hs]) + _dot(p[1].astype(BF16), v_refs[1][:, hs])
             + _dot(p[2].astype(BF16), v_refs[2][:, hs]))
        o_ref[:, hs] = (o / l).astype(BF16)


def _attention(u3, bias_tile, *, heads, head_dim):
    b, s, _ = u3.shape
    width = heads * head_dim
    nblk = s // ATT_Q
    kern = functools.partial(_attn_kernel, heads=heads, head_dim=head_dim)

    def kv_spec(col, back):
        return pl.BlockSpec((None, ATT_Q, width),
                            lambda bi, i: (bi, jnp.maximum(i - back, 0), col))

    return pl.pallas_call(
        kern,
        grid=(b, nblk),
        in_specs=[
            pl.BlockSpec((None, ATT_Q, width), lambda bi, i: (bi, i, 0)),
            kv_spec(1, 2), kv_spec(1, 1), kv_spec(1, 0),
            kv_spec(2, 2), kv_spec(2, 1), kv_spec(2, 0),
            _const_spec(bias_tile.shape),
        ],
        out_specs=pl.BlockSpec((None, ATT_Q, width), lambda bi, i: (bi, i, 0)),
        out_shape=jax.ShapeDtypeStruct((b, s, width), BF16),
        compiler_params=_params(("parallel", "parallel")),
        name="band_attn",
    )(u3, u3, u3, u3, u3, u3, u3, bias_tile)


def _attn_bias_tile(rel_bias):
    pad = A_LEFT_CHUNKS * CHUNK
    r = jnp.arange(ATT_Q)[:, None]
    c = jnp.arange(ATT_K)[None, :]
    rel = jnp.clip(r + pad - c, -REL_CLIP, REL_CLIP) + REL_CLIP
    bias = jnp.transpose(rel_bias[rel], (2, 0, 1)).astype(F32)
    qc, kc = r // CHUNK, c // CHUNK
    allowed = (kc >= qc) & (kc <= qc + A_LEFT_CHUNKS)
    return jnp.where(allowed[None], bias, MASK_NEG)


def _log_sigmoid(z):
    return jnp.minimum(z, 0.0) - jnp.log1p(jnp.exp(-jnp.abs(z)))


def _gla_kernel(q_ref, k_ref, v_ref, r_ref, lr_ref, gup_ref, gbias_ref, ogain_ref, tri_ref,
                y_ref, state_ref, *, heads, dk, dv):
    @pl.when(pl.program_id(1) == 0)
    def _():
        state_ref[...] = jnp.zeros_like(state_ref)

    rows = q_ref.shape[0]
    qk_w = heads * dk
    row_id = lax.broadcasted_iota(jnp.int32, (CHUNK, qk_w), 0)
    ci = lax.broadcasted_iota(jnp.int32, (CHUNK, CHUNK), 0)
    cj = lax.broadcasted_iota(jnp.int32, (CHUNK, CHUNK), 1)
    causal = ci >= cj
    n_sub = CHUNK // SUB

    for c in range(rows // CHUNK):
        rs = slice(c * CHUNK, (c + 1) * CHUNK)
        z = jnp.dot(lr_ref[rs, :], gup_ref[...], precision=HIGHEST,
                    preferred_element_type=F32) + gbias_ref[...]
        log_a = _log_sigmoid(z) * (1.0 / B_GATE_TEMP)
        b = jnp.dot(tri_ref[...], log_a, precision=HIGHEST, preferred_element_type=F32)
        q = q_ref[rs, :].astype(F32) * (dk ** -0.5)
        k = k_ref[rs, :].astype(F32)
        b_last = b[CHUNK - 1:CHUNK, :]

        ref_rows = [b[(j + 1) * SUB - 1:(j + 1) * SUB, :] for j in range(n_sub)]
        ref_full = jnp.concatenate(
            [jnp.broadcast_to(rj, (SUB, qk_w)) for rj in ref_rows], axis=0)
        k_dec = k * jnp.exp(ref_full - b)
        q_dec = [q * jnp.exp(jnp.where(row_id >= j * SUB, b - ref_rows[j], -jnp.inf))
                 for j in range(n_sub)]
        k_sub = [jnp.where((row_id >= j * SUB) & (row_id < (j + 1) * SUB), k_dec, 0.0)
                 for j in range(n_sub)]
        q_in = (q * jnp.exp(b)).astype(BF16)
        k_out = (k * jnp.exp(b_last - b)).astype(BF16)
        e_last = jnp.exp(b_last)

        for h in range(heads):
            ks = slice(h * dk, (h + 1) * dk)
            vs = slice(h * dv, (h + 1) * dv)
            q_cat = jnp.concatenate([qd[:, ks] for qd in q_dec], axis=1).astype(BF16)
            k_cat = jnp.concatenate([kz[:, ks] for kz in k_sub], axis=1).astype(BF16)
            attn = jnp.where(causal, _dot_nt(q_cat, k_cat), 0.0)
            v_h = v_ref[rs, vs]
            st = state_ref[h]
            o = _dot(attn.astype(BF16), v_h) + _dot_nt(q_in[:, ks], st.astype(BF16))
            state_ref[h] = st * e_last[:, ks] + _dot_tn(v_h, k_out[:, ks])
            r_h = r_ref[rs, vs].astype(F32)
            y = _rms(o, ogain_ref[:, vs]) * (r_h * jax.nn.sigmoid(r_h))
            y_ref[rs, vs] = y.astype(BF16)


def _gla(u3, lr3, gup_pad, gbias, ogain, tri, *, heads, dk, dv, qk_col, v_col, r_col):
    b, s, _ = u3.shape
    qk_w, v_w = heads * dk, heads * dv
    kern = functools.partial(_gla_kernel, heads=heads, dk=dk, dv=dv)
    return pl.pallas_call(
        kern,
        grid=(b, s // GLA_ROWS),
        in_specs=[
            pl.BlockSpec((None, GLA_ROWS, qk_w), lambda bi, i: (bi, i, qk_col)),
            pl.BlockSpec((None, GLA_ROWS, qk_w), lambda bi, i: (bi, i, qk_col + 1)),
            pl.BlockSpec((None, GLA_ROWS, v_w), lambda bi, i: (bi, i, v_col)),
            pl.BlockSpec((None, GLA_ROWS, v_w), lambda bi, i: (bi, i, r_col)),
            pl.BlockSpec((None, GLA_ROWS, LANES), lambda bi, i: (bi, i, 0)),
            _const_spec(gup_pad.shape),
            _const_spec(gbias.shape),
            _const_spec(ogain.shape),
            _const_spec(tri.shape),
        ],
        out_specs=pl.BlockSpec((None, GLA_ROWS, v_w), lambda bi, i: (bi, i, 0)),
        out_shape=jax.ShapeDtypeStruct((b, s, v_w), BF16),
        scratch_shapes=[pltpu.VMEM((heads, dv, dk), F32)],
        compiler_params=_params(("parallel", "arbitrary")),
        name="gla",
    )(u3, u3, u3, u3, lr3, gup_pad, gbias, ogain, tri)


def _route(logits):
    lane = lax.broadcasted_iota(jnp.int32, logits.shape, 1)
    neg_inf = -jnp.inf
    gl = jnp.where(lane < N_GROUPS, logits, neg_inf)
    gmax = jnp.max(gl, axis=-1, keepdims=True)
    g_idx = jnp.min(jnp.where(gl == gmax, lane, LANES), axis=-1, keepdims=True)
    g_top = 1.0 / jnp.sum(jnp.exp(gl - gmax), axis=-1, keepdims=True)
    e_lane = lane - N_GROUPS
    in_group = (e_lane >= g_idx * EXPERTS_PER_GROUP) & (e_lane < (g_idx + 1) * EXPERTS_PER_GROUP)
    el = jnp.where(in_group, logits, neg_inf)
    e1 = jnp.max(el, axis=-1, keepdims=True)
    i1 = jnp.min(jnp.where(el == e1, lane, LANES), axis=-1, keepdims=True)
    el2 = jnp.where(lane == i1, neg_inf, el)
    e2 = jnp.max(el2, axis=-1, keepdims=True)
    i2 = jnp.min(jnp.where(el2 == e2, lane, LANES), axis=-1, keepdims=True)
    t = jnp.exp(e2 - e1)
    w1 = g_top / (1.0 + t)
    w2 = g_top * t / (1.0 + t)
    return lane, i1, i2, w1, w2


R_E1, R_E2, R_W1, R_W2, R_RANK1, R_RANK2 = range(6)


def _post_kernel(x_ref, ya_ref, yb_ref, ga_ref, gb_ref, pa_ref, pb_ref, wo_ref, gain_ref,
                 wr_ref, br_ref, tri_ref, xmid_ref, h_ref, rinfo_ref, counts_ref):
    @pl.when(pl.program_id(0) == 0)
    def _():
        counts_ref[...] = jnp.zeros_like(counts_ref)

    ga = jax.nn.sigmoid(ga_ref[...].astype(F32))
    gb = jax.nn.sigmoid(gb_ref[...].astype(F32))
    merged = ga * _dot(ya_ref[...], pa_ref[...]) + gb * _dot(yb_ref[...], pb_ref[...])
    x_mid = x_ref[...] + _dot(merged.astype(BF16), wo_ref[...])
    xmid_ref[...] = x_mid
    h = _rms(x_mid, gain_ref[...])
    h_ref[...] = h
    logits = jnp.dot(h, wr_ref[...], precision=HIGHEST, preferred_element_type=F32) + br_ref[...]
    lane, i1, i2, w1, w2 = _route(logits)

    onehot = jnp.where((lane == i1) | (lane == i2), 1.0, 0.0)
    total = _dot(tri_ref[...], onehot.astype(BF16)) + counts_ref[...]
    rank1 = jnp.sum(jnp.where(lane == i1, total, 0.0), axis=-1, keepdims=True)
    rank2 = jnp.sum(jnp.where(lane == i2, total, 0.0), axis=-1, keepdims=True)
    counts_ref[...] += jnp.sum(onehot, axis=0, keepdims=True)

    rec = jnp.zeros(logits.shape, F32)
    for slot, val in ((R_E1, i1.astype(F32)), (R_E2, i2.astype(F32)), (R_W1, w1), (R_W2, w2),
                      (R_RANK1, rank1), (R_RANK2, rank2)):
        rec = jnp.where(lane == slot, val, rec)
    rinfo_ref[...] = rec


def _post(x2d, ya, yb, u2d, pa, pb, wo, gain, wr, br, tri, *, tm, ga_col, gb_col):
    t, d = x2d.shape
    aw, bw = ya.shape[1], yb.shape[1]
    return pl.pallas_call(
        _post_kernel,
        grid=(t // tm,),
        in_specs=[
            pl.BlockSpec((tm, d), lambda i: (i, 0)),
            pl.BlockSpec((tm, aw), lambda i: (i, 0)),
            pl.BlockSpec((tm, bw), lambda i: (i, 0)),
            pl.BlockSpec((tm, d), lambda i: (i, ga_col)),
            pl.BlockSpec((tm, d), lambda i: (i, gb_col)),
            _const_spec(pa.shape), _const_spec(pb.shape), _const_spec(wo.shape),
            _const_spec(gain.shape), _const_spec(wr.shape), _const_spec(br.shape),
            _const_spec(tri.shape),
        ],
        out_specs=[
            pl.BlockSpec((tm, d), lambda i: (i, 0)),
            pl.BlockSpec((tm, d), lambda i: (i, 0)),
            pl.BlockSpec((tm, LANES), lambda i: (i, 0)),
            pl.BlockSpec((1, LANES), lambda i: (0, 0)),
        ],
        out_shape=[
            jax.ShapeDtypeStruct((t, d), F32),
            jax.ShapeDtypeStruct((t, d), F32),
            jax.ShapeDtypeStruct((t, LANES), F32),
            jax.ShapeDtypeStruct((1, LANES), F32),
        ],
        compiler_params=_params(("arbitrary",)),
        name="merge_route",
    )(x2d, ya, yb, u2d, u2d, pa, pb, wo, gain, wr, br, tri)


MOE_TILE = 256
COMBINE_TILE = 256


def _row_copy(src_hbm, row, dst, dst_row, sem):
    return pltpu.make_async_copy(src_hbm.at[pl.ds(row, 1)], dst.at[pl.ds(dst_row, 1)], sem)


def _moe_kernel(tile_expert_ref, src_cur_ref, src_nxt_ref, h_hbm, wg_ref, wu_ref, wd_ref,
                ys_ref, buf_ref, sem_ref):
    del tile_expert_ref
    i = pl.program_id(0)
    n_tiles = pl.num_programs(0)
    slot = i % 2
    rows = ys_ref.shape[0]

    def start_gather(src_ref, s):
        def body(r, carry):
            _row_copy(h_hbm, src_ref[0, 0, r], buf_ref.at[s], r, sem_ref.at[s]).start()
            return carry
        lax.fori_loop(0, rows, body, 0, unroll=8)

    @pl.when(i == 0)
    def _():
        start_gather(src_cur_ref, 0)

    @pl.when(i + 1 < n_tiles)
    def _():
        start_gather(src_nxt_ref, 1 - slot)

    def wait_body(r, carry):
        _row_copy(h_hbm, 0, buf_ref.at[slot], r, sem_ref.at[slot]).wait()
        return carry
    lax.fori_loop(0, rows, wait_body, 0, unroll=8)

    x = buf_ref[slot].astype(BF16)
    a = _dot(x, wg_ref[...])
    a = a * jax.nn.sigmoid(a) * _dot(x, wu_ref[...])
    ys_ref[...] = _dot(a.astype(BF16), wd_ref[...])


def _moe_experts(tile_expert, src3, h, wg, wu, wd):
    n_tiles = src3.shape[0]
    _, d = h.shape
    _, _, de = wg.shape
    grid_spec = pltpu.PrefetchScalarGridSpec(
        num_scalar_prefetch=1,
        grid=(n_tiles,),
        in_specs=[
            pl.BlockSpec((1, 1, MOE_TILE), lambda i, te: (i, 0, 0), memory_space=pltpu.SMEM),
            pl.BlockSpec((1, 1, MOE_TILE), lambda i, te: (jnp.minimum(i + 1, n_tiles - 1), 0, 0),
                         memory_space=pltpu.SMEM),
            pl.BlockSpec(memory_space=pl.ANY),
            pl.BlockSpec((None, d, de), lambda i, te: (te[i], 0, 0)),
            pl.BlockSpec((None, d, de), lambda i, te: (te[i], 0, 0)),
            pl.BlockSpec((None, de, d), lambda i, te: (te[i], 0, 0)),
        ],
        out_specs=pl.BlockSpec((MOE_TILE, d), lambda i, te: (i, 0)),
        scratch_shapes=[pltpu.VMEM((2, MOE_TILE, d), F32), pltpu.SemaphoreType.DMA((2,))],
    )
    return pl.pallas_call(
        _moe_kernel,
        grid_spec=grid_spec,
        out_shape=jax.ShapeDtypeStruct((n_tiles * MOE_TILE, d), F32),
        compiler_params=_params(("arbitrary",)),
        name="moe_experts",
    )(tile_expert, src3, src3, h, wg, wu, wd)


def _combine_kernel(pos_ref, xmid_ref, rinfo_ref, ys_hbm, o_ref, b1_ref, b2_ref, sem_ref):
    rows = o_ref.shape[0]

    def copies(r):
        return (_row_copy(ys_hbm, pos_ref[0, 0, r], b1_ref, r, sem_ref.at[0]),
                _row_copy(ys_hbm, pos_ref[0, 0, rows + r], b2_ref, r, sem_ref.at[1]))

    def start_body(r, carry):
        c1, c2 = copies(r)
        c1.start()
        c2.start()
        return carry
    lax.fori_loop(0, rows, start_body, 0, unroll=8)

    def wait_body(r, carry):
        c1, c2 = copies(r)
        c1.wait()
        c2.wait()
        return carry
    lax.fori_loop(0, rows, wait_body, 0, unroll=8)

    rinfo = rinfo_ref[...]
    w1 = rinfo[:, R_W1:R_W1 + 1]
    w2 = rinfo[:, R_W2:R_W2 + 1]
    o_ref[...] = xmid_ref[...] + w1 * b1_ref[...] + w2 * b2_ref[...]


def _combine(pos3, xmid, rinfo, ys):
    t, d = xmid.shape
    tc = COMBINE_TILE
    return pl.pallas_call(
        _combine_kernel,
        grid=(t // tc,),
        in_specs=[
            pl.BlockSpec((1, 1, 2 * tc), lambda i: (i, 0, 0), memory_space=pltpu.SMEM),
            pl.BlockSpec((tc, d), lambda i: (i, 0)),
            pl.BlockSpec((tc, LANES), lambda i: (i, 0)),
            pl.BlockSpec(memory_space=pl.ANY),
        ],
        out_specs=pl.BlockSpec((tc, d), lambda i: (i, 0)),
        out_shape=jax.ShapeDtypeStruct((t, d), F32),
        scratch_shapes=[pltpu.VMEM((tc, d), F32), pltpu.VMEM((tc, d), F32),
                        pltpu.SemaphoreType.DMA((2,))],
        compiler_params=_params(("arbitrary",)),
        name="moe_combine",
    )(pos3, xmid, rinfo, ys)


def _moe(x_mid, h2, rinfo, counts_row, wg, wu, wd):
    t = x_mid.shape[0]
    ne = wg.shape[0]
    n_tiles = (2 * t) // MOE_TILE + ne
    e_id = rinfo[:, R_E1:R_E2 + 1].astype(jnp.int32) - N_GROUPS
    rank = rinfo[:, R_RANK1:R_RANK2 + 1].astype(jnp.int32)
    counts = counts_row[0, N_GROUPS:N_GROUPS + ne].astype(jnp.int32)
    padded = (counts + MOE_TILE - 1) // MOE_TILE * MOE_TILE
    ends = jnp.cumsum(padded)
    pos = (ends - padded)[e_id] + rank
    token = jnp.broadcast_to(jnp.arange(t, dtype=jnp.int32)[:, None], (t, 2))
    src = jnp.zeros((n_tiles * MOE_TILE,), jnp.int32).at[pos.reshape(-1)].set(
        token.reshape(-1), unique_indices=True)
    tile_start = jnp.arange(n_tiles, dtype=jnp.int32) * MOE_TILE
    tile_expert = jnp.minimum(
        jnp.sum((ends[None, :] <= tile_start[:, None]).astype(jnp.int32), axis=1), ne - 1)
    ys = _moe_experts(tile_expert, src.reshape(n_tiles, 1, MOE_TILE), h2, wg, wu, wd)
    tc = COMBINE_TILE
    pos3 = jnp.transpose(pos.reshape(t // tc, tc, 2), (0, 2, 1)).reshape(t // tc, 1, 2 * tc)
    return _combine(pos3, x_mid, rinfo, ys)


def kernel(x, norm_mix_gain, w_in, a_q_gain, a_k_gain, rel_bias, b_gate_up, b_gate_bias,
           b_out_gain, proj_a, proj_b, w_out, norm_ffn_gain, w_group, b_group, w_router,
           b_router, w_gate, w_up, w_down):
    bsz, seq, d = x.shape
    depth = w_in.shape[0]
    t = bsz * seq
    a_width = proj_a.shape[1]
    a_hd = a_width // A_HEADS
    b_vw = proj_b.shape[1]
    b_qkw = b_gate_up.shape[2]
    dk, dv = b_qkw // B_HEADS, b_vw // B_HEADS
    assert seq % ATT_Q == 0 and seq % GLA_ROWS == 0 and d == 2 * a_width == 2 * b_vw
    assert b_qkw * 2 == a_width and N_GROUPS + N_EXPERTS <= LANES

    o_lr = 3 * a_width + 2 * b_qkw + 2 * b_vw
    qk_col = 3 * a_width // b_qkw
    v_col = (3 * a_width + 2 * b_qkw) // b_vw
    r_col = v_col + 1
    ga_col = (3 * a_width + 2 * b_qkw + 2 * b_vw) // d
    gb_col = ga_col + 1
    assert (3 * a_width + 2 * b_qkw + 2 * b_vw) % d == 0

    head_id = jnp.arange(a_width) // a_hd
    ones_bd = (head_id[:, None] == head_id[None, :]).astype(BF16)
    tri = jnp.tril(jnp.ones((CHUNK, CHUNK), F32))
    bias_tile = _attn_bias_tile(rel_bias)

    x2d = x.reshape(t, d)
    tm = 512
    tri_tm = jnp.tril(jnp.ones((tm, tm), BF16), k=-1)
    for l in range(depth):
        w_main = jnp.concatenate([w_in[l][:, :o_lr], w_in[l][:, o_lr + B_GATE_RANK:]],
                                 axis=1).astype(BF16)
        w_lr = jnp.pad(w_in[l][:, o_lr:o_lr + B_GATE_RANK],
                       ((0, 0), (0, LANES - B_GATE_RANK))).astype(BF16)
        u, lr = _inproj(x2d, norm_mix_gain[l][None, :], w_main, w_lr,
                        a_q_gain[l].reshape(1, a_width), a_k_gain[l].reshape(1, a_width),
                        ones_bd, tm=tm, a_width=a_width, head_dim=a_hd)
        u3 = u.reshape(bsz, seq, u.shape[1])
        y_a = _attention(u3, bias_tile, heads=A_HEADS, head_dim=a_hd)
        gup_pad = jnp.pad(b_gate_up[l], ((0, LANES - B_GATE_RANK), (0, 0)))
        y_b = _gla(u3, lr.reshape(bsz, seq, LANES), gup_pad, b_gate_bias[l][None, :],
                   b_out_gain[l].reshape(1, b_vw), tri, heads=B_HEADS, dk=dk, dv=dv,
                   qk_col=qk_col, v_col=v_col, r_col=r_col)
        w_r = jnp.pad(jnp.concatenate([w_group[l], w_router[l]], axis=1),
                      ((0, 0), (0, LANES - N_GROUPS - N_EXPERTS)))
        b_r = jnp.pad(jnp.concatenate([b_group[l], b_router[l]]),
                      (0, LANES - N_GROUPS - N_EXPERTS))[None, :]
        x_mid, h2, rinfo, counts = _post(
            x2d, y_a.reshape(t, a_width), y_b.reshape(t, b_vw), u, proj_a[l].astype(BF16),
            proj_b[l].astype(BF16), w_out[l].astype(BF16), norm_ffn_gain[l][None, :], w_r, b_r,
            tri_tm, tm=tm, ga_col=ga_col, gb_col=gb_col)
        x2d = _moe(x_mid, h2, rinfo, counts, w_gate[l].astype(BF16), w_up[l].astype(BF16),
                   w_down[l].astype(BF16))
    return x2d.reshape(bsz, seq, d)
```

```python
import functools

import jax
import jax.numpy as jnp
from jax import lax
from jax.experimental import pallas as pl
from jax.experimental.pallas import tpu as pltpu

F32 = jnp.float32
BF16 = jnp.bfloat16
HIGHEST = lax.Precision.HIGHEST

EPS = 1e-6
CHUNK = 64
A_HEADS = 8
A_LEFT_CHUNKS = 8
REL_CLIP = 2 * CHUNK
B_HEADS = 4
B_GATE_RANK = 16
B_GATE_TEMP = 16.0
N_GROUPS = 4
EXPERTS_PER_GROUP = 8
N_EXPERTS = N_GROUPS * EXPERTS_PER_GROUP

LANES = 128
MASK_NEG = -1e30
VMEM_LIMIT = 56 * 1024 * 1024

ATT_Q = 256
ATT_K = ATT_Q + A_LEFT_CHUNKS * CHUNK
GLA_ROWS = 256
SUB = 16


def _params(sem):
    return pltpu.CompilerParams(dimension_semantics=sem, vmem_limit_bytes=VMEM_LIMIT)


def _const_spec(shape):
    nd = len(shape)
    return pl.BlockSpec(shape, lambda *_: (0,) * nd, pipeline_mode=pl.Buffered(1))


def _dot(a, b):
    return jnp.dot(a, b, preferred_element_type=F32)


def _dot_nt(a, b):
    return lax.dot_general(a, b, (((1,), (1,)), ((), ())), preferred_element_type=F32)


def _dot_tn(a, b):
    return lax.dot_general(a, b, (((0,), (0,)), ((), ())), preferred_element_type=F32)


def _rms(x, gain):
    return x * lax.rsqrt(jnp.mean(x * x, axis=-1, keepdims=True) + EPS) * gain


def _head_norm(acc, ones_bd, gain, head_dim, scale):
    sq = acc * acc
    hi = sq.astype(BF16)
    lo = (sq - hi.astype(F32)).astype(BF16)
    ss = _dot(hi, ones_bd) + _dot(lo, ones_bd)
    inv = lax.rsqrt(ss * (1.0 / head_dim) + EPS)
    return acc * inv * (gain * scale)


def _inproj_kernel(x_ref, gain_ref, w_ref, wlr_ref, qg_ref, kg_ref, ones_ref,
                   u_ref, lr_ref, *, a_width, head_dim, col_chunk):
    x = x_ref[...]
    h = _rms(x, gain_ref[...]).astype(BF16)
    n_cols = w_ref.shape[1]
    for c0 in range(0, n_cols, col_chunk):
        acc = _dot(h, w_ref[:, c0:c0 + col_chunk])
        if c0 == 0:
            acc = _head_norm(acc, ones_ref[...], qg_ref[...], head_dim, head_dim ** -0.5)
        elif c0 == a_width:
            acc = _head_norm(acc, ones_ref[...], kg_ref[...], head_dim, 1.0)
        u_ref[:, c0:c0 + col_chunk] = acc.astype(BF16)
    lr_ref[...] = _dot(h, wlr_ref[...])


def _inproj(x2d, gain, w_main, w_lr, q_gain, k_gain, ones_bd, *, tm, a_width, head_dim):
    t, d = x2d.shape
    n = w_main.shape[1]
    kern = functools.partial(_inproj_kernel, a_width=a_width, head_dim=head_dim,
                             col_chunk=a_width)
    return pl.pallas_call(
        kern,
        grid=(t // tm,),
        in_specs=[
            pl.BlockSpec((tm, d), lambda i: (i, 0)),
            _const_spec((1, d)),
            _const_spec((d, n)),
            _const_spec((d, LANES)),
            _const_spec((1, a_width)),
            _const_spec((1, a_width)),
            _const_spec((a_width, a_width)),
        ],
        out_specs=[
            pl.BlockSpec((tm, n), lambda i: (i, 0)),
            pl.BlockSpec((tm, LANES), lambda i: (i, 0)),
        ],
        out_shape=[
            jax.ShapeDtypeStruct((t, n), BF16),
            jax.ShapeDtypeStruct((t, LANES), F32),
        ],
        compiler_params=_params(("parallel",)),
        name="inproj",
    )(x2d, gain, w_main, w_lr, q_gain, k_gain, ones_bd)


def _attn_kernel(q_ref, k0_ref, k1_ref, k2_ref, v0_ref, v1_ref, v2_ref, bias_ref, o_ref,
                 *, heads, head_dim):
    i = pl.program_id(1)
    pen = (jnp.where(i >= 2, 0.0, MASK_NEG), jnp.where(i >= 1, 0.0, MASK_NEG), None)
    k_refs = (k0_ref, k1_ref, k2_ref)
    v_refs = (v0_ref, v1_ref, v2_ref)
    nq = q_ref.shape[0]
    for h in range(heads):
        hs = slice(h * head_dim, (h + 1) * head_dim)
        q = q_ref[:, hs]
        s = []
        for j in range(3):
            sj = _dot_nt(q, k_refs[j][:, hs]) + bias_ref[h, :, j * nq:(j + 1) * nq]
            if pen[j] is not None:
                sj = sj + pen[j]
            s.append(sj)
        m = jnp.maximum(jnp.maximum(jnp.max(s[0], axis=-1, keepdims=True),
                                    jnp.max(s[1], axis=-1, keepdims=True)),
                        jnp.max(s[2], axis=-1, keepdims=True))
        p = [jnp.exp(sj - m) for sj in s]
        l = (jnp.sum(p[0], axis=-1, keepdims=True) + jnp.sum(p[1], axis=-1, keepdims=True)
             + jnp.sum(p[2], axis=-1, keepdims=True))
        o = (_dot(p[0].astype(BF16), v_refs[0][:, hs]) + _dot(p[1].astype(BF16), v_refs[1][:, hs])
             + _dot(p[2].astype(BF16), v_refs[2][:, hs]))
        o_ref[:, hs] = (o / l).astype(BF16)


def _attention(u3, bias_tile, *, heads, head_dim):
    b, s, _ = u3.shape
    width = heads * head_dim
    nblk = s // ATT_Q
    kern = functools.partial(_attn_kernel, heads=heads, head_dim=head_dim)

    def kv_spec(col, back):
        return pl.BlockSpec((None, ATT_Q, width),
                            lambda bi, i: (bi, jnp.maximum(i - back, 0), col))

    return pl.pallas_call(
        kern,
        grid=(b, nblk),
        in_specs=[
            pl.BlockSpec((None, ATT_Q, width), lambda bi, i: (bi, i, 0)),
            kv_spec(1, 2), kv_spec(1, 1), kv_spec(1, 0),
            kv_spec(2, 2), kv_spec(2, 1), kv_spec(2, 0),
            _const_spec(bias_tile.shape),
        ],
        out_specs=pl.BlockSpec((None, ATT_Q, width), lambda bi, i: (bi, i, 0)),
        out_shape=jax.ShapeDtypeStruct((b, s, width), BF16),
        compiler_params=_params(("parallel", "parallel")),
        name="band_attn",
    )(u3, u3, u3, u3, u3, u3, u3, bias_tile)


def _attn_bias_tile(rel_bias):
    pad = A_LEFT_CHUNKS * CHUNK
    r = jnp.arange(ATT_Q)[:, None]
    c = jnp.arange(ATT_K)[None, :]
    heads = rel_bias.shape[1]
    d_max, d_min = ATT_Q - 1 + pad, pad - (ATT_K - 1)
    rb = rel_bias.astype(F32).T
    g = jnp.concatenate([
        jnp.broadcast_to(rb[:, -1:], (heads, d_max - REL_CLIP)),
        rb[:, ::-1],
        jnp.broadcast_to(rb[:, :1], (heads, -d_min - REL_CLIP + 1)),
    ], axis=1)
    length = g.shape[1]
    skew = jnp.broadcast_to(g[:, None, :], (heads, ATT_Q + 1, length)).reshape(heads, -1)
    skew = skew[:, :ATT_Q * (length + 1)].reshape(heads, ATT_Q, length + 1)
    bias = skew[:, ::-1, :ATT_K]
    qc, kc = r // CHUNK, c // CHUNK
    allowed = (kc >= qc) & (kc <= qc + A_LEFT_CHUNKS)
    return jnp.where(allowed[None], bias, MASK_NEG)


def _log_sigmoid(z):
    return jnp.minimum(z, 0.0) - jnp.log1p(jnp.exp(-jnp.abs(z)))


def _gla_kernel(q_ref, k_ref, v_ref, r_ref, lr_ref, gup_ref, gbias_ref, ogain_ref, tri_ref,
                y_ref, state_ref, *, heads, dk, dv):
    @pl.when(pl.program_id(1) == 0)
    def _():
        state_ref[...] = jnp.zeros_like(state_ref)

    rows = q_ref.shape[0]
    qk_w = heads * dk
    row_id = lax.broadcasted_iota(jnp.int32, (CHUNK, qk_w), 0)
    ci = lax.broadcasted_iota(jnp.int32, (CHUNK, CHUNK), 0)
    cj = lax.broadcasted_iota(jnp.int32, (CHUNK, CHUNK), 1)
    causal = ci >= cj
    n_sub = CHUNK // SUB

    for c in range(rows // CHUNK):
        rs = slice(c * CHUNK, (c + 1) * CHUNK)
        z = jnp.dot(lr_ref[rs, :], gup_ref[...], precision=HIGHEST,
                    preferred_element_type=F32) + gbias_ref[...]
        log_a = _log_sigmoid(z) * (1.0 / B_GATE_TEMP)
        b = jnp.dot(tri_ref[...], log_a, precision=HIGHEST, preferred_element_type=F32)
        q = q_ref[rs, :].astype(F32) * (dk ** -0.5)
        k = k_ref[rs, :].astype(F32)
        b_last = b[CHUNK - 1:CHUNK, :]

        ref_rows = [b[(j + 1) * SUB - 1:(j + 1) * SUB, :] for j in range(n_sub)]
        ref_full = jnp.concatenate(
            [jnp.broadcast_to(rj, (SUB, qk_w)) for rj in ref_rows], axis=0)
        k_dec = k * jnp.exp(ref_full - b)
        q_dec = [q * jnp.exp(jnp.where(row_id >= j * SUB, b - ref_rows[j], -jnp.inf))
                 for j in range(n_sub)]
        k_sub = [jnp.where((row_id >= j * SUB) & (row_id < (j + 1) * SUB), k_dec, 0.0)
                 for j in range(n_sub)]
        q_in = (q * jnp.exp(b)).astype(BF16)
        k_out = (k * jnp.exp(b_last - b)).astype(BF16)
        e_last = jnp.exp(b_last)

        for h in range(heads):
            ks = slice(h * dk, (h + 1) * dk)
            vs = slice(h * dv, (h + 1) * dv)
            q_cat = jnp.concatenate([qd[:, ks] for qd in q_dec], axis=1).astype(BF16)
            k_cat = jnp.concatenate([kz[:, ks] for kz in k_sub], axis=1).astype(BF16)
            attn = jnp.where(causal, _dot_nt(q_cat, k_cat), 0.0)
            v_h = v_ref[rs, vs]
            st = state_ref[h]
            o = _dot(attn.astype(BF16), v_h) + _dot_nt(q_in[:, ks], st.astype(BF16))
            state_ref[h] = st * e_last[:, ks] + _dot_tn(v_h, k_out[:, ks])
            r_h = r_ref[rs, vs].astype(F32)
            y = _rms(o, ogain_ref[:, vs]) * (r_h * jax.nn.sigmoid(r_h))
            y_ref[rs, vs] = y.astype(BF16)


def _gla(u3, lr3, gup_pad, gbias, ogain, tri, *, heads, dk, dv, qk_col, v_col, r_col):
    b, s, _ = u3.shape
    qk_w, v_w = heads * dk, heads * dv
    kern = functools.partial(_gla_kernel, heads=heads, dk=dk, dv=dv)
    return pl.pallas_call(
        kern,
        grid=(b, s // GLA_ROWS),
        in_specs=[
            pl.BlockSpec((None, GLA_ROWS, qk_w), lambda bi, i: (bi, i, qk_col)),
            pl.BlockSpec((None, GLA_ROWS, qk_w), lambda bi, i: (bi, i, qk_col + 1)),
            pl.BlockSpec((None, GLA_ROWS, v_w), lambda bi, i: (bi, i, v_col)),
            pl.BlockSpec((None, GLA_ROWS, v_w), lambda bi, i: (bi, i, r_col)),
            pl.BlockSpec((None, GLA_ROWS, LANES), lambda bi, i: (bi, i, 0)),
            _const_spec(gup_pad.shape),
            _const_spec(gbias.shape),
            _const_spec(ogain.shape),
            _const_spec(tri.shape),
        ],
        out_specs=pl.BlockSpec((None, GLA_ROWS, v_w), lambda bi, i: (bi, i, 0)),
        out_shape=jax.ShapeDtypeStruct((b, s, v_w), BF16),
        scratch_shapes=[pltpu.VMEM((heads, dv, dk), F32)],
        compiler_params=_params(("parallel", "arbitrary")),
        name="gla",
    )(u3, u3, u3, u3, lr3, gup_pad, gbias, ogain, tri)


def _route(logits):
    lane = lax.broadcasted_iota(jnp.int32, logits.shape, 1)
    neg_inf = -jnp.inf
    gl = jnp.where(lane < N_GROUPS, logits, neg_inf)
    gmax = jnp.max(gl, axis=-1, keepdims=True)
    g_idx = jnp.min(jnp.where(gl == gmax, lane, LANES), axis=-1, keepdims=True)
    g_top = 1.0 / jnp.sum(jnp.exp(gl - gmax), axis=-1, keepdims=True)
    e_lane = lane - N_GROUPS
    in_group = (e_lane >= g_idx * EXPERTS_PER_GROUP) & (e_lane < (g_idx + 1) * EXPERTS_PER_GROUP)
    el = jnp.where(in_group, logits, neg_inf)
    e1 = jnp.max(el, axis=-1, keepdims=True)
    i1 = jnp.min(jnp.where(el == e1, lane, LANES), axis=-1, keepdims=True)
    el2 = jnp.where(lane == i1, neg_inf, el)
    e2 = jnp.max(el2, axis=-1, keepdims=True)
    i2 = jnp.min(jnp.where(el2 == e2, lane, LANES), axis=-1, keepdims=True)
    t = jnp.exp(e2 - e1)
    w1 = g_top / (1.0 + t)
    w2 = g_top * t / (1.0 + t)
    return lane, i1, i2, w1, w2


R_E1, R_E2, R_W1, R_W2, R_RANK1, R_RANK2 = range(6)


def _post_kernel(x_ref, ya_ref, yb_ref, ga_ref, gb_ref, pa_ref, pb_ref, wo_ref, gain_ref,
                 wr_ref, br_ref, tri_ref, xmid_ref, h_ref, rinfo_ref, counts_ref):
    @pl.when(pl.program_id(0) == 0)
    def _():
        counts_ref[...] = jnp.zeros_like(counts_ref)

    ga = jax.nn.sigmoid(ga_ref[...].astype(F32))
    gb = jax.nn.sigmoid(gb_ref[...].astype(F32))
    merged = ga * _dot(ya_ref[...], pa_ref[...]) + gb * _dot(yb_ref[...], pb_ref[...])
    x_mid = x_ref[...] + _dot(merged.astype(BF16), wo_ref[...])
    xmid_ref[...] = x_mid
    h = _rms(x_mid, gain_ref[...])
    h_ref[...] = h
    logits = jnp.dot(h, wr_ref[...], precision=HIGHEST, preferred_element_type=F32) + br_ref[...]
    lane, i1, i2, w1, w2 = _route(logits)

    onehot = jnp.where((lane == i1) | (lane == i2), 1.0, 0.0)
    total = _dot(tri_ref[...], onehot.astype(BF16)) + counts_ref[...]
    rank1 = jnp.sum(jnp.where(lane == i1, total, 0.0), axis=-1, keepdims=True)
    rank2 = jnp.sum(jnp.where(lane == i2, total, 0.0), axis=-1, keepdims=True)
    counts_ref[...] += jnp.sum(onehot, axis=0, keepdims=True)

    rec = jnp.zeros(logits.shape, F32)
    for slot, val in ((R_E1, i1.astype(F32)), (R_E2, i2.astype(F32)), (R_W1, w1), (R_W2, w2),
                      (R_RANK1, rank1), (R_RANK2, rank2)):
        rec = jnp.where(lane == slot, val, rec)
    rinfo_ref[...] = rec


def _post(x2d, ya, yb, u2d, pa, pb, wo, gain, wr, br, tri, *, tm, ga_col, gb_col):
    t, d = x2d.shape
    aw, bw = ya.shape[1], yb.shape[1]
    return pl.pallas_call(
        _post_kernel,
        grid=(t // tm,),
        in_specs=[
            pl.BlockSpec((tm, d), lambda i: (i, 0)),
            pl.BlockSpec((tm, aw), lambda i: (i, 0)),
            pl.BlockSpec((tm, bw), lambda i: (i, 0)),
            pl.BlockSpec((tm, d), lambda i: (i, ga_col)),
            pl.BlockSpec((tm, d), lambda i: (i, gb_col)),
            _const_spec(pa.shape), _const_spec(pb.shape), _const_spec(wo.shape),
            _const_spec(gain.shape), _const_spec(wr.shape), _const_spec(br.shape),
            _const_spec(tri.shape),
        ],
        out_specs=[
            pl.BlockSpec((tm, d), lambda i: (i, 0)),
            pl.BlockSpec((tm, d), lambda i: (i, 0)),
            pl.BlockSpec((tm, LANES), lambda i: (i, 0)),
            pl.BlockSpec((1, LANES), lambda i: (0, 0)),
        ],
        out_shape=[
            jax.ShapeDtypeStruct((t, d), F32),
            jax.ShapeDtypeStruct((t, d), F32),
            jax.ShapeDtypeStruct((t, LANES), F32),
            jax.ShapeDtypeStruct((1, LANES), F32),
        ],
        compiler_params=_params(("arbitrary",)),
        name="merge_route",
    )(x2d, ya, yb, u2d, u2d, pa, pb, wo, gain, wr, br, tri)


MOE_TILE = 256
COMBINE_TILE = 256


def _row_copy(src_hbm, row, dst, dst_row, sem):
    return pltpu.make_async_copy(src_hbm.at[pl.ds(row, 1)], dst.at[pl.ds(dst_row, 1)], sem)


def _moe_kernel(tile_expert_ref, src_cur_ref, src_nxt_ref, h_hbm, wg_ref, wu_ref, wd_ref,
                ys_ref, buf_ref, sem_ref, xb_ref):
    del tile_expert_ref
    i = pl.program_id(0)
    n_tiles = pl.num_programs(0)
    slot = i % 2
    rows = ys_ref.shape[0]

    def start_gather(src_ref, s):
        for r in range(rows):
            _row_copy(h_hbm, src_ref[0, 0, r], buf_ref.at[s], r, sem_ref.at[s]).start()

    @pl.when(i == 0)
    def _():
        start_gather(src_cur_ref, 0)

    pltpu.make_async_copy(h_hbm.at[pl.ds(0, rows)], buf_ref.at[slot], sem_ref.at[slot]).wait()
    xb_ref[...] = buf_ref[slot].astype(BF16)
    start_gather(src_nxt_ref, 1 - slot)

    x = xb_ref[...]
    a = _dot(x, wg_ref[...])
    a = a * jax.nn.sigmoid(a) * _dot(x, wu_ref[...])
    ys_ref[...] = _dot(a.astype(BF16), wd_ref[...])

    @pl.when(i == n_tiles - 1)
    def _():
        pltpu.make_async_copy(h_hbm.at[pl.ds(0, rows)], buf_ref.at[1 - slot],
                              sem_ref.at[1 - slot]).wait()


def _moe_experts(tile_expert, src3, h, wg, wu, wd):
    n_tiles = src3.shape[0]
    _, d = h.shape
    _, _, de = wg.shape
    grid_spec = pltpu.PrefetchScalarGridSpec(
        num_scalar_prefetch=1,
        grid=(n_tiles,),
        in_specs=[
            pl.BlockSpec((1, 1, MOE_TILE), lambda i, te: (i, 0, 0), memory_space=pltpu.SMEM),
            pl.BlockSpec((1, 1, MOE_TILE), lambda i, te: (jnp.minimum(i + 1, n_tiles - 1), 0, 0),
                         memory_space=pltpu.SMEM),
            pl.BlockSpec(memory_space=pl.ANY),
            pl.BlockSpec((None, d, de), lambda i, te: (te[i], 0, 0)),
            pl.BlockSpec((None, d, de), lambda i, te: (te[i], 0, 0)),
            pl.BlockSpec((None, de, d), lambda i, te: (te[i], 0, 0)),
        ],
        out_specs=pl.BlockSpec((MOE_TILE, d), lambda i, te: (i, 0)),
        scratch_shapes=[pltpu.VMEM((2, MOE_TILE, d), F32), pltpu.SemaphoreType.DMA((2,)),
                        pltpu.VMEM((MOE_TILE, d), BF16)],
    )
    return pl.pallas_call(
        _moe_kernel,
        grid_spec=grid_spec,
        out_shape=jax.ShapeDtypeStruct((n_tiles * MOE_TILE, d), F32),
        compiler_params=_params(("arbitrary",)),
        name="moe_experts",
    )(tile_expert, src3, src3, h, wg, wu, wd)


def _combine_kernel(pos_ref, xmid_ref, rinfo_ref, ys_hbm, o_ref, b1_ref, b2_ref, sem_ref):
    rows = o_ref.shape[0]

    def copies(r):
        return (_row_copy(ys_hbm, pos_ref[0, 0, r], b1_ref, r, sem_ref.at[0]),
                _row_copy(ys_hbm, pos_ref[0, 0, rows + r], b2_ref, r, sem_ref.at[1]))

    for r in range(rows):
        c1, c2 = copies(r)
        c1.start()
        c2.start()
    pltpu.make_async_copy(ys_hbm.at[pl.ds(0, rows)], b1_ref, sem_ref.at[0]).wait()
    pltpu.make_async_copy(ys_hbm.at[pl.ds(0, rows)], b2_ref, sem_ref.at[1]).wait()

    rinfo = rinfo_ref[...]
    w1 = rinfo[:, R_W1:R_W1 + 1]
    w2 = rinfo[:, R_W2:R_W2 + 1]
    o_ref[...] = xmid_ref[...] + w1 * b1_ref[...] + w2 * b2_ref[...]


def _combine(pos3, xmid, rinfo, ys):
    t, d = xmid.shape
    tc = COMBINE_TILE
    return pl.pallas_call(
        _combine_kernel,
        grid=(t // tc,),
        in_specs=[
            pl.BlockSpec((1, 1, 2 * tc), lambda i: (i, 0, 0), memory_space=pltpu.SMEM),
            pl.BlockSpec((tc, d), lambda i: (i, 0)),
            pl.BlockSpec((tc, LANES), lambda i: (i, 0)),
            pl.BlockSpec(memory_space=pl.ANY),
        ],
        out_specs=pl.BlockSpec((tc, d), lambda i: (i, 0)),
        out_shape=jax.ShapeDtypeStruct((t, d), F32),
        scratch_shapes=[pltpu.VMEM((tc, d), F32), pltpu.VMEM((tc, d), F32),
                        pltpu.SemaphoreType.DMA((2,))],
        compiler_params=_params(("arbitrary",)),
        name="moe_combine",
    )(pos3, xmid, rinfo, ys)


def _moe(x_mid, h2, rinfo, counts_row, wg, wu, wd):
    t = x_mid.shape[0]
    ne = wg.shape[0]
    n_tiles = (2 * t) // MOE_TILE + ne
    e_id = rinfo[:, R_E1:R_E2 + 1].astype(jnp.int32) - N_GROUPS
    rank = rinfo[:, R_RANK1:R_RANK2 + 1].astype(jnp.int32)
    counts = counts_row[0, N_GROUPS:N_GROUPS + ne].astype(jnp.int32)
    padded = (counts + MOE_TILE - 1) // MOE_TILE * MOE_TILE
    ends = jnp.cumsum(padded)
    pos = (ends - padded)[e_id] + rank
    token = jnp.broadcast_to(jnp.arange(t, dtype=jnp.int32)[:, None], (t, 2))
    src = jnp.zeros((n_tiles * MOE_TILE,), jnp.int32).at[pos.reshape(-1)].set(
        token.reshape(-1), unique_indices=True)
    tile_start = jnp.arange(n_tiles, dtype=jnp.int32) * MOE_TILE
    tile_expert = jnp.minimum(
        jnp.sum((ends[None, :] <= tile_start[:, None]).astype(jnp.int32), axis=1), ne - 1)
    ys = _moe_experts(tile_expert, src.reshape(n_tiles, 1, MOE_TILE), h2, wg, wu, wd)
    tc = COMBINE_TILE
    pos3 = jnp.transpose(pos.reshape(t // tc, tc, 2), (0, 2, 1)).reshape(t // tc, 1, 2 * tc)
    return _combine(pos3, x_mid, rinfo, ys)


def kernel(x, norm_mix_gain, w_in, a_q_gain, a_k_gain, rel_bias, b_gate_up, b_gate_bias,
           b_out_gain, proj_a, proj_b, w_out, norm_ffn_gain, w_group, b_group, w_router,
           b_router, w_gate, w_up, w_down):
    bsz, seq, d = x.shape
    depth = w_in.shape[0]
    t = bsz * seq
    a_width = proj_a.shape[1]
    a_hd = a_width // A_HEADS
    b_vw = proj_b.shape[1]
    b_qkw = b_gate_up.shape[2]
    dk, dv = b_qkw // B_HEADS, b_vw // B_HEADS
    assert seq % ATT_Q == 0 and seq % GLA_ROWS == 0 and d == 2 * a_width == 2 * b_vw
    assert b_qkw * 2 == a_width and N_GROUPS + N_EXPERTS <= LANES

    o_lr = 3 * a_width + 2 * b_qkw + 2 * b_vw
    qk_col = 3 * a_width // b_qkw
    v_col = (3 * a_width + 2 * b_qkw) // b_vw
    r_col = v_col + 1
    ga_col = (3 * a_width + 2 * b_qkw + 2 * b_vw) // d
    gb_col = ga_col + 1
    assert (3 * a_width + 2 * b_qkw + 2 * b_vw) % d == 0

    head_id = jnp.arange(a_width) // a_hd
    ones_bd = (head_id[:, None] == head_id[None, :]).astype(BF16)
    tri = jnp.tril(jnp.ones((CHUNK, CHUNK), F32))
    bias_tile = _attn_bias_tile(rel_bias)

    x2d = x.reshape(t, d)
    tm = 512
    tri_tm = jnp.tril(jnp.ones((tm, tm), BF16), k=-1)
    for l in range(depth):
        w_main = jnp.concatenate([w_in[l][:, :o_lr], w_in[l][:, o_lr + B_GATE_RANK:]],
                                 axis=1).astype(BF16)
        w_lr = jnp.pad(w_in[l][:, o_lr:o_lr + B_GATE_RANK],
                       ((0, 0), (0, LANES - B_GATE_RANK))).astype(BF16)
        u, lr = _inproj(x2d, norm_mix_gain[l][None, :], w_main, w_lr,
                        a_q_gain[l].reshape(1, a_width), a_k_gain[l].reshape(1, a_width),
                        ones_bd, tm=tm, a_width=a_width, head_dim=a_hd)
        u3 = u.reshape(bsz, seq, u.shape[1])
        y_a = _attention(u3, bias_tile, heads=A_HEADS, head_dim=a_hd)
        gup_pad = jnp.pad(b_gate_up[l], ((0, LANES - B_GATE_RANK), (0, 0)))
        y_b = _gla(u3, lr.reshape(bsz, seq, LANES), gup_pad, b_gate_bias[l][None, :],
                   b_out_gain[l].reshape(1, b_vw), tri, heads=B_HEADS, dk=dk, dv=dv,
                   qk_col=qk_col, v_col=v_col, r_col=r_col)
        w_r = jnp.pad(jnp.concatenate([w_group[l], w_router[l]], axis=1),
                      ((0, 0), (0, LANES - N_GROUPS - N_EXPERTS)))
        b_r = jnp.pad(jnp.concatenate([b_group[l], b_router[l]]),
                      (0, LANES - N_GROUPS - N_EXPERTS))[None, :]
        x_mid, h2, rinfo, counts = _post(
            x2d, y_a.reshape(t, a_width), y_b.reshape(t, b_vw), u, proj_a[l].astype(BF16),
            proj_b[l].astype(BF16), w_out[l].astype(BF16), norm_ffn_gain[l][None, :], w_r, b_r,
            tri_tm, tm=tm, ga_col=ga_col, gb_col=gb_col)
        x2d = _moe(x_mid, h2, rinfo, counts, w_gate[l].astype(BF16), w_up[l].astype(BF16),
                   w_down[l].astype(BF16))
    return x2d.reshape(bsz, seq, d)
```

```python
import functools

import jax
import jax.numpy as jnp
from jax import lax
from jax.experimental import pallas as pl
from jax.experimental.pallas import tpu as pltpu

F32 = jnp.float32
BF16 = jnp.bfloat16
HIGHEST = lax.Precision.HIGHEST

EPS = 1e-6
CHUNK = 64
A_HEADS = 8
A_LEFT_CHUNKS = 8
REL_CLIP = 2 * CHUNK
B_HEADS = 4
B_GATE_RANK = 16
B_GATE_TEMP = 16.0
N_GROUPS = 4
EXPERTS_PER_GROUP = 8
N_EXPERTS = N_GROUPS * EXPERTS_PER_GROUP

LANES = 128
MASK_NEG = -1e30
VMEM_LIMIT = 56 * 1024 * 1024

ATT_Q = 256
ATT_K = ATT_Q + A_LEFT_CHUNKS * CHUNK
GLA_ROWS = 256
SUB = 16


def _params(sem):
    return pltpu.CompilerParams(dimension_semantics=sem, vmem_limit_bytes=VMEM_LIMIT)


def _const_spec(shape):
    nd = len(shape)
    return pl.BlockSpec(shape, lambda *_: (0,) * nd, pipeline_mode=pl.Buffered(1))


def _dot(a, b):
    return jnp.dot(a, b, preferred_element_type=F32)


def _dot_nt(a, b):
    return lax.dot_general(a, b, (((1,), (1,)), ((), ())), preferred_element_type=F32)


def _dot_tn(a, b):
    return lax.dot_general(a, b, (((0,), (0,)), ((), ())), preferred_element_type=F32)


def _rms(x, gain):
    return x * lax.rsqrt(jnp.mean(x * x, axis=-1, keepdims=True) + EPS) * gain


def _head_norm(acc, ones_bd, gain, head_dim, scale):
    sq = acc * acc
    hi = sq.astype(BF16)
    lo = (sq - hi.astype(F32)).astype(BF16)
    ss = _dot(hi, ones_bd) + _dot(lo, ones_bd)
    inv = lax.rsqrt(ss * (1.0 / head_dim) + EPS)
    return acc * inv * (gain * scale)


def _inproj_kernel(x_ref, gain_ref, w_ref, wlr_ref, qg_ref, kg_ref, ones_ref,
                   u_ref, lr_ref, *, a_width, head_dim, col_chunk):
    x = x_ref[...]
    h = _rms(x, gain_ref[...]).astype(BF16)
    n_cols = w_ref.shape[1]
    for c0 in range(0, n_cols, col_chunk):
        acc = _dot(h, w_ref[:, c0:c0 + col_chunk])
        if c0 == 0:
            acc = _head_norm(acc, ones_ref[...], qg_ref[...], head_dim, head_dim ** -0.5)
        elif c0 == a_width:
            acc = _head_norm(acc, ones_ref[...], kg_ref[...], head_dim, 1.0)
        u_ref[:, c0:c0 + col_chunk] = acc.astype(BF16)
    lr_ref[...] = _dot(h, wlr_ref[...])


def _inproj(x2d, gain, w_main, w_lr, q_gain, k_gain, ones_bd, *, tm, a_width, head_dim):
    t, d = x2d.shape
    n = w_main.shape[1]
    kern = functools.partial(_inproj_kernel, a_width=a_width, head_dim=head_dim,
                             col_chunk=a_width)
    return pl.pallas_call(
        kern,
        grid=(t // tm,),
        in_specs=[
            pl.BlockSpec((tm, d), lambda i: (i, 0)),
            _const_spec((1, d)),
            _const_spec((d, n)),
            _const_spec((d, LANES)),
            _const_spec((1, a_width)),
            _const_spec((1, a_width)),
            _const_spec((a_width, a_width)),
        ],
        out_specs=[
            pl.BlockSpec((tm, n), lambda i: (i, 0)),
            pl.BlockSpec((tm, LANES), lambda i: (i, 0)),
        ],
        out_shape=[
            jax.ShapeDtypeStruct((t, n), BF16),
            jax.ShapeDtypeStruct((t, LANES), F32),
        ],
        compiler_params=_params(("parallel",)),
        name="inproj",
    )(x2d, gain, w_main, w_lr, q_gain, k_gain, ones_bd)


def _attn_kernel(q_ref, k0_ref, k1_ref, k2_ref, v0_ref, v1_ref, v2_ref, bias_ref, o_ref,
                 *, heads, head_dim):
    i = pl.program_id(1)
    pen = (jnp.where(i >= 2, 0.0, MASK_NEG), jnp.where(i >= 1, 0.0, MASK_NEG), None)
    k_refs = (k0_ref, k1_ref, k2_ref)
    v_refs = (v0_ref, v1_ref, v2_ref)
    nq = q_ref.shape[0]
    for h in range(heads):
        hs = slice(h * head_dim, (h + 1) * head_dim)
        q = q_ref[:, hs]
        s = []
        for j in range(3):
            sj = _dot_nt(q, k_refs[j][:, hs]) + bias_ref[h, :, j * nq:(j + 1) * nq]
            if pen[j] is not None:
                sj = sj + pen[j]
            s.append(sj)
        m = jnp.maximum(jnp.maximum(jnp.max(s[0], axis=-1, keepdims=True),
                                    jnp.max(s[1], axis=-1, keepdims=True)),
                        jnp.max(s[2], axis=-1, keepdims=True))
        p = [jnp.exp(sj - m) for sj in s]
        l = (jnp.sum(p[0], axis=-1, keepdims=True) + jnp.sum(p[1], axis=-1, keepdims=True)
             + jnp.sum(p[2], axis=-1, keepdims=True))
        o = (_dot(p[0].astype(BF16), v_refs[0][:, hs]) + _dot(p[1].astype(BF16), v_refs[1][:, hs])
             + _dot(p[2].astype(BF16), v_refs[2][:, hs]))
        o_ref[:, hs] = (o / l).astype(BF16)


def _attention(u3, bias_tile, *, heads, head_dim):
    b, s, _ = u3.shape
    width = heads * head_dim
    nblk = s // ATT_Q
    kern = functools.partial(_attn_kernel, heads=heads, head_dim=head_dim)

    def kv_spec(col, back):
        return pl.BlockSpec((None, ATT_Q, width),
                            lambda bi, i: (bi, jnp.maximum(i - back, 0), col))

    return pl.pallas_call(
        kern,
        grid=(b, nblk),
        in_specs=[
            pl.BlockSpec((None, ATT_Q, width), lambda bi, i: (bi, i, 0)),
            kv_spec(1, 2), kv_spec(1, 1), kv_spec(1, 0),
            kv_spec(2, 2), kv_spec(2, 1), kv_spec(2, 0),
            _const_spec(bias_tile.shape),
        ],
        out_specs=pl.BlockSpec((None, ATT_Q, width), lambda bi, i: (bi, i, 0)),
        out_shape=jax.ShapeDtypeStruct((b, s, width), BF16),
        compiler_params=_params(("parallel", "parallel")),
        name="band_attn",
    )(u3, u3, u3, u3, u3, u3, u3, bias_tile)


def _attn_bias_tile(rel_bias):
    pad = A_LEFT_CHUNKS * CHUNK
    r = jnp.arange(ATT_Q)[:, None]
    c = jnp.arange(ATT_K)[None, :]
    heads = rel_bias.shape[1]
    d_max, d_min = ATT_Q - 1 + pad, pad - (ATT_K - 1)
    rb = rel_bias.astype(F32).T
    g = jnp.concatenate([
        jnp.broadcast_to(rb[:, -1:], (heads, d_max - REL_CLIP)),
        rb[:, ::-1],
        jnp.broadcast_to(rb[:, :1], (heads, -d_min - REL_CLIP + 1)),
    ], axis=1)
    length = g.shape[1]
    skew = jnp.broadcast_to(g[:, None, :], (heads, ATT_Q + 1, length)).reshape(heads, -1)
    skew = skew[:, :ATT_Q * (length + 1)].reshape(heads, ATT_Q, length + 1)
    bias = skew[:, ::-1, :ATT_K]
    qc, kc = r // CHUNK, c // CHUNK
    allowed = (kc >= qc) & (kc <= qc + A_LEFT_CHUNKS)
    return jnp.where(allowed[None], bias, MASK_NEG)


def _log_sigmoid(z):
    return jnp.minimum(z, 0.0) - jnp.log1p(jnp.exp(-jnp.abs(z)))


def _gla_kernel(q_ref, k_ref, v_ref, r_ref, lr_ref, gup_ref, gbias_ref, ogain_ref, tri_ref,
                y_ref, state_ref, *, heads, dk, dv):
    @pl.when(pl.program_id(1) == 0)
    def _():
        state_ref[...] = jnp.zeros_like(state_ref)

    rows = q_ref.shape[0]
    qk_w = heads * dk
    row_id = lax.broadcasted_iota(jnp.int32, (CHUNK, qk_w), 0)
    ci = lax.broadcasted_iota(jnp.int32, (CHUNK, CHUNK), 0)
    cj = lax.broadcasted_iota(jnp.int32, (CHUNK, CHUNK), 1)
    causal = ci >= cj
    n_sub = CHUNK // SUB

    for c in range(rows // CHUNK):
        rs = slice(c * CHUNK, (c + 1) * CHUNK)
        z = jnp.dot(lr_ref[rs, :], gup_ref[...], precision=HIGHEST,
                    preferred_element_type=F32) + gbias_ref[...]
        log_a = _log_sigmoid(z) * (1.0 / B_GATE_TEMP)
        b = jnp.dot(tri_ref[...], log_a, precision=HIGHEST, preferred_element_type=F32)
        q = q_ref[rs, :].astype(F32) * (dk ** -0.5)
        k = k_ref[rs, :].astype(F32)
        b_last = b[CHUNK - 1:CHUNK, :]

        ref_rows = [b[(j + 1) * SUB - 1:(j + 1) * SUB, :] for j in range(n_sub)]
        ref_full = jnp.concatenate(
            [jnp.broadcast_to(rj, (SUB, qk_w)) for rj in ref_rows], axis=0)
        k_dec = k * jnp.exp(ref_full - b)
        q_dec = [q * jnp.exp(jnp.where(row_id >= j * SUB, b - ref_rows[j], -jnp.inf))
                 for j in range(n_sub)]
        k_sub = [jnp.where((row_id >= j * SUB) & (row_id < (j + 1) * SUB), k_dec, 0.0)
                 for j in range(n_sub)]
        q_in = (q * jnp.exp(b)).astype(BF16)
        k_out = (k * jnp.exp(b_last - b)).astype(BF16)
        e_last = jnp.exp(b_last)

        for h in range(heads):
            ks = slice(h * dk, (h + 1) * dk)
            vs = slice(h * dv, (h + 1) * dv)
            q_cat = jnp.concatenate([qd[:, ks] for qd in q_dec], axis=1).astype(BF16)
            k_cat = jnp.concatenate([kz[:, ks] for kz in k_sub], axis=1).astype(BF16)
            attn = jnp.where(causal, _dot_nt(q_cat, k_cat), 0.0)
            v_h = v_ref[rs, vs]
            st = state_ref[h]
            o = _dot(attn.astype(BF16), v_h) + _dot_nt(q_in[:, ks], st.astype(BF16))
            state_ref[h] = st * e_last[:, ks] + _dot_tn(v_h, k_out[:, ks])
            r_h = r_ref[rs, vs].astype(F32)
            y = _rms(o, ogain_ref[:, vs]) * (r_h * jax.nn.sigmoid(r_h))
            y_ref[rs, vs] = y.astype(BF16)


def _gla(u3, lr3, gup_pad, gbias, ogain, tri, *, heads, dk, dv, qk_col, v_col, r_col):
    b, s, _ = u3.shape
    qk_w, v_w = heads * dk, heads * dv
    kern = functools.partial(_gla_kernel, heads=heads, dk=dk, dv=dv)
    return pl.pallas_call(
        kern,
        grid=(b, s // GLA_ROWS),
        in_specs=[
            pl.BlockSpec((None, GLA_ROWS, qk_w), lambda bi, i: (bi, i, qk_col)),
            pl.BlockSpec((None, GLA_ROWS, qk_w), lambda bi, i: (bi, i, qk_col + 1)),
            pl.BlockSpec((None, GLA_ROWS, v_w), lambda bi, i: (bi, i, v_col)),
            pl.BlockSpec((None, GLA_ROWS, v_w), lambda bi, i: (bi, i, r_col)),
            pl.BlockSpec((None, GLA_ROWS, LANES), lambda bi, i: (bi, i, 0)),
            _const_spec(gup_pad.shape),
            _const_spec(gbias.shape),
            _const_spec(ogain.shape),
            _const_spec(tri.shape),
        ],
        out_specs=pl.BlockSpec((None, GLA_ROWS, v_w), lambda bi, i: (bi, i, 0)),
        out_shape=jax.ShapeDtypeStruct((b, s, v_w), BF16),
        scratch_shapes=[pltpu.VMEM((heads, dv, dk), F32)],
        compiler_params=_params(("parallel", "arbitrary")),
        name="gla",
    )(u3, u3, u3, u3, lr3, gup_pad, gbias, ogain, tri)


def _route(logits):
    lane = lax.broadcasted_iota(jnp.int32, logits.shape, 1)
    neg_inf = -jnp.inf
    gl = jnp.where(lane < N_GROUPS, logits, neg_inf)
    gmax = jnp.max(gl, axis=-1, keepdims=True)
    g_idx = jnp.min(jnp.where(gl == gmax, lane, LANES), axis=-1, keepdims=True)
    g_top = 1.0 / jnp.sum(jnp.exp(gl - gmax), axis=-1, keepdims=True)
    e_lane = lane - N_GROUPS
    in_group = (e_lane >= g_idx * EXPERTS_PER_GROUP) & (e_lane < (g_idx + 1) * EXPERTS_PER_GROUP)
    el = jnp.where(in_group, logits, neg_inf)
    e1 = jnp.max(el, axis=-1, keepdims=True)
    i1 = jnp.min(jnp.where(el == e1, lane, LANES), axis=-1, keepdims=True)
    el2 = jnp.where(lane == i1, neg_inf, el)
    e2 = jnp.max(el2, axis=-1, keepdims=True)
    i2 = jnp.min(jnp.where(el2 == e2, lane, LANES), axis=-1, keepdims=True)
    t = jnp.exp(e2 - e1)
    w1 = g_top / (1.0 + t)
    w2 = g_top * t / (1.0 + t)
    return lane, i1, i2, w1, w2


R_E1, R_E2, R_W1, R_W2, R_RANK1, R_RANK2 = range(6)


def _post_kernel(x_ref, ya_ref, yb_ref, ga_ref, gb_ref, pa_ref, pb_ref, wo_ref, gain_ref,
                 wr_ref, br_ref, tri_ref, xmid_ref, h_ref, rinfo_ref, counts_ref):
    @pl.when(pl.program_id(0) == 0)
    def _():
        counts_ref[...] = jnp.zeros_like(counts_ref)

    ga = jax.nn.sigmoid(ga_ref[...].astype(F32))
    gb = jax.nn.sigmoid(gb_ref[...].astype(F32))
    merged = ga * _dot(ya_ref[...], pa_ref[...]) + gb * _dot(yb_ref[...], pb_ref[...])
    x_mid = x_ref[...] + _dot(merged.astype(BF16), wo_ref[...])
    xmid_ref[...] = x_mid
    h = _rms(x_mid, gain_ref[...])
    _store_row_tiles(h_ref, h)
    logits = jnp.dot(h, wr_ref[...], precision=HIGHEST, preferred_element_type=F32) + br_ref[...]
    lane, i1, i2, w1, w2 = _route(logits)

    onehot = jnp.where((lane == i1) | (lane == i2), 1.0, 0.0)
    total = _dot(tri_ref[...], onehot.astype(BF16)) + counts_ref[...]
    rank1 = jnp.sum(jnp.where(lane == i1, total, 0.0), axis=-1, keepdims=True)
    rank2 = jnp.sum(jnp.where(lane == i2, total, 0.0), axis=-1, keepdims=True)
    counts_ref[...] += jnp.sum(onehot, axis=0, keepdims=True)

    rec = jnp.zeros(logits.shape, F32)
    for slot, val in ((R_E1, i1.astype(F32)), (R_E2, i2.astype(F32)), (R_W1, w1), (R_W2, w2),
                      (R_RANK1, rank1), (R_RANK2, rank2)):
        rec = jnp.where(lane == slot, val, rec)
    rinfo_ref[...] = rec


def _post(x2d, ya, yb, u2d, pa, pb, wo, gain, wr, br, tri, *, tm, ga_col, gb_col):
    t, d = x2d.shape
    aw, bw = ya.shape[1], yb.shape[1]
    return pl.pallas_call(
        _post_kernel,
        grid=(t // tm,),
        in_specs=[
            pl.BlockSpec((tm, d), lambda i: (i, 0)),
            pl.BlockSpec((tm, aw), lambda i: (i, 0)),
            pl.BlockSpec((tm, bw), lambda i: (i, 0)),
            pl.BlockSpec((tm, d), lambda i: (i, ga_col)),
            pl.BlockSpec((tm, d), lambda i: (i, gb_col)),
            _const_spec(pa.shape), _const_spec(pb.shape), _const_spec(wo.shape),
            _const_spec(gain.shape), _const_spec(wr.shape), _const_spec(br.shape),
            _const_spec(tri.shape),
        ],
        out_specs=[
            pl.BlockSpec((tm, d), lambda i: (i, 0)),
            pl.BlockSpec((tm, d // LANES, LANES), lambda i: (i, 0, 0)),
            pl.BlockSpec((tm, LANES), lambda i: (i, 0)),
            pl.BlockSpec((1, LANES), lambda i: (0, 0)),
        ],
        out_shape=[
            jax.ShapeDtypeStruct((t, d), F32),
            jax.ShapeDtypeStruct((t, d // LANES, LANES), F32),
            jax.ShapeDtypeStruct((t, LANES), F32),
            jax.ShapeDtypeStruct((1, LANES), F32),
        ],
        compiler_params=_params(("arbitrary",)),
        name="merge_route",
    )(x2d, ya, yb, u2d, u2d, pa, pb, wo, gain, wr, br, tri)


MOE_TILE = 256
COMBINE_TILE = 256


def _store_row_tiles(ref, val):
    for s in range(ref.shape[-2]):
        ref[:, s, :] = val[:, s * LANES:(s + 1) * LANES]


def _load_row_tiles(ref):
    return [ref[:, s, :] for s in range(ref.shape[-2])]


def _row_copy(src_hbm, row, dst, dst_row, sem):
    return pltpu.make_async_copy(src_hbm.at[pl.ds(row, 1)], dst.at[pl.ds(dst_row, 1)], sem)


def _moe_kernel(tile_expert_ref, src_cur_ref, src_nxt_ref, h_hbm, wg_ref, wu_ref, wd_ref,
                ys_ref, buf_ref, sem_ref, xb_ref):
    del tile_expert_ref
    i = pl.program_id(0)
    n_tiles = pl.num_programs(0)
    slot = i % 2
    rows = ys_ref.shape[0]

    def start_gather(src_ref, s):
        for r in range(rows):
            _row_copy(h_hbm, src_ref[0, 0, r], buf_ref.at[s], r, sem_ref.at[s]).start()

    @pl.when(i == 0)
    def _():
        start_gather(src_cur_ref, 0)

    pltpu.make_async_copy(h_hbm.at[pl.ds(0, rows)], buf_ref.at[slot], sem_ref.at[slot]).wait()
    for s, slab in enumerate(_load_row_tiles(buf_ref.at[slot])):
        xb_ref[:, s * LANES:(s + 1) * LANES] = slab.astype(BF16)
    start_gather(src_nxt_ref, 1 - slot)

    x = xb_ref[...]
    a = _dot(x, wg_ref[...])
    a = a * jax.nn.sigmoid(a) * _dot(x, wu_ref[...])
    _store_row_tiles(ys_ref, _dot(a.astype(BF16), wd_ref[...]))

    @pl.when(i == n_tiles - 1)
    def _():
        pltpu.make_async_copy(h_hbm.at[pl.ds(0, rows)], buf_ref.at[1 - slot],
                              sem_ref.at[1 - slot]).wait()


def _moe_experts(tile_expert, src3, h, wg, wu, wd):
    n_tiles = src3.shape[0]
    _, sub, _ = h.shape
    d = sub * LANES
    _, _, de = wg.shape
    grid_spec = pltpu.PrefetchScalarGridSpec(
        num_scalar_prefetch=1,
        grid=(n_tiles,),
        in_specs=[
            pl.BlockSpec((1, 1, MOE_TILE), lambda i, te: (i, 0, 0), memory_space=pltpu.SMEM),
            pl.BlockSpec((1, 1, MOE_TILE), lambda i, te: (jnp.minimum(i + 1, n_tiles - 1), 0, 0),
                         memory_space=pltpu.SMEM),
            pl.BlockSpec(memory_space=pl.ANY),
            pl.BlockSpec((None, d, de), lambda i, te: (te[i], 0, 0)),
            pl.BlockSpec((None, d, de), lambda i, te: (te[i], 0, 0)),
            pl.BlockSpec((None, de, d), lambda i, te: (te[i], 0, 0)),
        ],
        out_specs=pl.BlockSpec((MOE_TILE, sub, LANES), lambda i, te: (i, 0, 0)),
        scratch_shapes=[pltpu.VMEM((2, MOE_TILE, sub, LANES), F32), pltpu.SemaphoreType.DMA((2,)),
                        pltpu.VMEM((MOE_TILE, d), BF16)],
    )
    return pl.pallas_call(
        _moe_kernel,
        grid_spec=grid_spec,
        out_shape=jax.ShapeDtypeStruct((n_tiles * MOE_TILE, sub, LANES), F32),
        compiler_params=_params(("arbitrary",)),
        name="moe_experts",
    )(tile_expert, src3, src3, h, wg, wu, wd)


def _combine_kernel(pos_ref, xmid_ref, rinfo_ref, ys_hbm, o_ref, b1_ref, b2_ref, sem_ref):
    rows = o_ref.shape[0]

    def copies(r):
        return (_row_copy(ys_hbm, pos_ref[0, 0, r], b1_ref, r, sem_ref.at[0]),
                _row_copy(ys_hbm, pos_ref[0, 0, rows + r], b2_ref, r, sem_ref.at[1]))

    for r in range(rows):
        c1, c2 = copies(r)
        c1.start()
        c2.start()
    pltpu.make_async_copy(ys_hbm.at[pl.ds(0, rows)], b1_ref, sem_ref.at[0]).wait()
    pltpu.make_async_copy(ys_hbm.at[pl.ds(0, rows)], b2_ref, sem_ref.at[1]).wait()

    rinfo = rinfo_ref[...]
    w1 = rinfo[:, R_W1:R_W1 + 1]
    w2 = rinfo[:, R_W2:R_W2 + 1]
    for s, (y1, y2) in enumerate(zip(_load_row_tiles(b1_ref), _load_row_tiles(b2_ref))):
        cols = slice(s * LANES, (s + 1) * LANES)
        o_ref[:, cols] = xmid_ref[:, cols] + w1 * y1 + w2 * y2


def _combine(pos3, xmid, rinfo, ys):
    t, d = xmid.shape
    tc = COMBINE_TILE
    return pl.pallas_call(
        _combine_kernel,
        grid=(t // tc,),
        in_specs=[
            pl.BlockSpec((1, 1, 2 * tc), lambda i: (i, 0, 0), memory_space=pltpu.SMEM),
            pl.BlockSpec((tc, d), lambda i: (i, 0)),
            pl.BlockSpec((tc, LANES), lambda i: (i, 0)),
            pl.BlockSpec(memory_space=pl.ANY),
        ],
        out_specs=pl.BlockSpec((tc, d), lambda i: (i, 0)),
        out_shape=jax.ShapeDtypeStruct((t, d), F32),
        scratch_shapes=[pltpu.VMEM((tc, d // LANES, LANES), F32),
                        pltpu.VMEM((tc, d // LANES, LANES), F32),
                        pltpu.SemaphoreType.DMA((2,))],
        compiler_params=_params(("arbitrary",)),
        name="moe_combine",
    )(pos3, xmid, rinfo, ys)


def _moe(x_mid, h2, rinfo, counts_row, wg, wu, wd):
    t = x_mid.shape[0]
    ne = wg.shape[0]
    n_tiles = (2 * t) // MOE_TILE + ne
    e_id = rinfo[:, R_E1:R_E2 + 1].astype(jnp.int32) - N_GROUPS
    rank = rinfo[:, R_RANK1:R_RANK2 + 1].astype(jnp.int32)
    counts = counts_row[0, N_GROUPS:N_GROUPS + ne].astype(jnp.int32)
    padded = (counts + MOE_TILE - 1) // MOE_TILE * MOE_TILE
    ends = jnp.cumsum(padded)
    pos = (ends - padded)[e_id] + rank
    token = jnp.broadcast_to(jnp.arange(t, dtype=jnp.int32)[:, None], (t, 2))
    src = jnp.zeros((n_tiles * MOE_TILE,), jnp.int32).at[pos.reshape(-1)].set(
        token.reshape(-1), unique_indices=True)
    tile_start = jnp.arange(n_tiles, dtype=jnp.int32) * MOE_TILE
    tile_expert = jnp.minimum(
        jnp.sum((ends[None, :] <= tile_start[:, None]).astype(jnp.int32), axis=1), ne - 1)
    ys = _moe_experts(tile_expert, src.reshape(n_tiles, 1, MOE_TILE), h2, wg, wu, wd)
    tc = COMBINE_TILE
    pos3 = jnp.transpose(pos.reshape(t // tc, tc, 2), (0, 2, 1)).reshape(t // tc, 1, 2 * tc)
    return _combine(pos3, x_mid, rinfo, ys)


def kernel(x, norm_mix_gain, w_in, a_q_gain, a_k_gain, rel_bias, b_gate_up, b_gate_bias,
           b_out_gain, proj_a, proj_b, w_out, norm_ffn_gain, w_group, b_group, w_router,
           b_router, w_gate, w_up, w_down):
    bsz, seq, d = x.shape
    depth = w_in.shape[0]
    t = bsz * seq
    a_width = proj_a.shape[1]
    a_hd = a_width // A_HEADS
    b_vw = proj_b.shape[1]
    b_qkw = b_gate_up.shape[2]
    dk, dv = b_qkw // B_HEADS, b_vw // B_HEADS
    assert seq % ATT_Q == 0 and seq % GLA_ROWS == 0 and d == 2 * a_width == 2 * b_vw
    assert b_qkw * 2 == a_width and N_GROUPS + N_EXPERTS <= LANES

    o_lr = 3 * a_width + 2 * b_qkw + 2 * b_vw
    qk_col = 3 * a_width // b_qkw
    v_col = (3 * a_width + 2 * b_qkw) // b_vw
    r_col = v_col + 1
    ga_col = (3 * a_width + 2 * b_qkw + 2 * b_vw) // d
    gb_col = ga_col + 1
    assert (3 * a_width + 2 * b_qkw + 2 * b_vw) % d == 0

    head_id = jnp.arange(a_width) // a_hd
    ones_bd = (head_id[:, None] == head_id[None, :]).astype(BF16)
    tri = jnp.tril(jnp.ones((CHUNK, CHUNK), F32))
    bias_tile = _attn_bias_tile(rel_bias)

    x2d = x.reshape(t, d)
    tm = 512
    tri_tm = jnp.tril(jnp.ones((tm, tm), BF16), k=-1)
    for l in range(depth):
        w_main = jnp.concatenate([w_in[l][:, :o_lr], w_in[l][:, o_lr + B_GATE_RANK:]],
                                 axis=1).astype(BF16)
        w_lr = jnp.pad(w_in[l][:, o_lr:o_lr + B_GATE_RANK],
                       ((0, 0), (0, LANES - B_GATE_RANK))).astype(BF16)
        u, lr = _inproj(x2d, norm_mix_gain[l][None, :], w_main, w_lr,
                        a_q_gain[l].reshape(1, a_width), a_k_gain[l].reshape(1, a_width),
                        ones_bd, tm=tm, a_width=a_width, head_dim=a_hd)
        u3 = u.reshape(bsz, seq, u.shape[1])
        y_a = _attention(u3, bias_tile, heads=A_HEADS, head_dim=a_hd)
        gup_pad = jnp.pad(b_gate_up[l], ((0, LANES - B_GATE_RANK), (0, 0)))
        y_b = _gla(u3, lr.reshape(bsz, seq, LANES), gup_pad, b_gate_bias[l][None, :],
                   b_out_gain[l].reshape(1, b_vw), tri, heads=B_HEADS, dk=dk, dv=dv,
                   qk_col=qk_col, v_col=v_col, r_col=r_col)
        w_r = jnp.pad(jnp.concatenate([w_group[l], w_router[l]], axis=1),
                      ((0, 0), (0, LANES - N_GROUPS - N_EXPERTS)))
        b_r = jnp.pad(jnp.concatenate([b_group[l], b_router[l]]),
                      (0, LANES - N_GROUPS - N_EXPERTS))[None, :]
        x_mid, h2, rinfo, counts = _post(
            x2d, y_a.reshape(t, a_width), y_b.reshape(t, b_vw), u, proj_a[l].astype(BF16),
            proj_b[l].astype(BF16), w_out[l].astype(BF16), norm_ffn_gain[l][None, :], w_r, b_r,
            tri_tm, tm=tm, ga_col=ga_col, gb_col=gb_col)
        x2d = _moe(x_mid, h2, rinfo, counts, w_gate[l].astype(BF16), w_up[l].astype(BF16),
                   w_down[l].astype(BF16))
    return x2d.reshape(bsz, seq, d)
```

```python
import functools

import jax
import jax.numpy as jnp
from jax import lax
from jax.experimental import pallas as pl
from jax.experimental.pallas import tpu as pltpu

F32 = jnp.float32
BF16 = jnp.bfloat16
HIGHEST = lax.Precision.HIGHEST

EPS = 1e-6
CHUNK = 64
A_HEADS = 8
A_LEFT_CHUNKS = 8
REL_CLIP = 2 * CHUNK
B_HEADS = 4
B_GATE_RANK = 16
B_GATE_TEMP = 16.0
N_GROUPS = 4
EXPERTS_PER_GROUP = 8
N_EXPERTS = N_GROUPS * EXPERTS_PER_GROUP

LANES = 128
MASK_NEG = -1e30
VMEM_LIMIT = 56 * 1024 * 1024

ATT_Q = 256
ATT_K = ATT_Q + A_LEFT_CHUNKS * CHUNK
GLA_ROWS = 256
SUB = 16


def _params(sem):
    return pltpu.CompilerParams(dimension_semantics=sem, vmem_limit_bytes=VMEM_LIMIT)


def _const_spec(shape):
    nd = len(shape)
    return pl.BlockSpec(shape, lambda *_: (0,) * nd, pipeline_mode=pl.Buffered(1))


def _dot(a, b):
    return jnp.dot(a, b, preferred_element_type=F32)


def _dot_nt(a, b):
    return lax.dot_general(a, b, (((1,), (1,)), ((), ())), preferred_element_type=F32)


def _dot_tn(a, b):
    return lax.dot_general(a, b, (((0,), (0,)), ((), ())), preferred_element_type=F32)


def _rms(x, gain):
    return x * lax.rsqrt(jnp.mean(x * x, axis=-1, keepdims=True) + EPS) * gain


def _head_norm(acc, ones_bd, gain, head_dim, scale):
    sq = acc * acc
    hi = sq.astype(BF16)
    lo = (sq - hi.astype(F32)).astype(BF16)
    ss = _dot(hi, ones_bd) + _dot(lo, ones_bd)
    inv = lax.rsqrt(ss * (1.0 / head_dim) + EPS)
    return acc * inv * (gain * scale)


def _inproj_kernel(x_ref, gain_ref, w_ref, wlr_ref, qg_ref, kg_ref, ones_ref,
                   u_ref, lr_ref, *, a_width, head_dim, col_chunk):
    x = x_ref[...]
    h = _rms(x, gain_ref[...]).astype(BF16)
    n_cols = w_ref.shape[1]
    for c0 in range(0, n_cols, col_chunk):
        acc = _dot(h, w_ref[:, c0:c0 + col_chunk])
        if c0 == 0:
            acc = _head_norm(acc, ones_ref[...], qg_ref[...], head_dim, head_dim ** -0.5)
        elif c0 == a_width:
            acc = _head_norm(acc, ones_ref[...], kg_ref[...], head_dim, 1.0)
        u_ref[:, c0:c0 + col_chunk] = acc.astype(BF16)
    lr_ref[...] = _dot(h, wlr_ref[...])


def _inproj(x2d, gain, w_main, w_lr, q_gain, k_gain, ones_bd, *, tm, a_width, head_dim):
    t, d = x2d.shape
    n = w_main.shape[1]
    kern = functools.partial(_inproj_kernel, a_width=a_width, head_dim=head_dim,
                             col_chunk=a_width)
    return pl.pallas_call(
        kern,
        grid=(t // tm,),
        in_specs=[
            pl.BlockSpec((tm, d), lambda i: (i, 0)),
            _const_spec((1, d)),
            _const_spec((d, n)),
            _const_spec((d, LANES)),
            _const_spec((1, a_width)),
            _const_spec((1, a_width)),
            _const_spec((a_width, a_width)),
        ],
        out_specs=[
            pl.BlockSpec((tm, n), lambda i: (i, 0)),
            pl.BlockSpec((tm, LANES), lambda i: (i, 0)),
        ],
        out_shape=[
            jax.ShapeDtypeStruct((t, n), BF16),
            jax.ShapeDtypeStruct((t, LANES), F32),
        ],
        compiler_params=_params(("parallel",)),
        name="inproj",
    )(x2d, gain, w_main, w_lr, q_gain, k_gain, ones_bd)


def _attn_kernel(q_ref, k0_ref, k1_ref, k2_ref, v0_ref, v1_ref, v2_ref, bias_ref, o_ref,
                 *, heads, head_dim):
    i = pl.program_id(1)
    pen = (jnp.where(i >= 2, 0.0, MASK_NEG), jnp.where(i >= 1, 0.0, MASK_NEG), None)
    k_refs = (k0_ref, k1_ref, k2_ref)
    v_refs = (v0_ref, v1_ref, v2_ref)
    nq = q_ref.shape[0]
    for h in range(heads):
        hs = slice(h * head_dim, (h + 1) * head_dim)
        q = q_ref[:, hs]
        s = []
        for j in range(3):
            sj = _dot_nt(q, k_refs[j][:, hs]) + bias_ref[h, :, j * nq:(j + 1) * nq]
            if pen[j] is not None:
                sj = sj + pen[j]
            s.append(sj)
        m = jnp.maximum(jnp.maximum(jnp.max(s[0], axis=-1, keepdims=True),
                                    jnp.max(s[1], axis=-1, keepdims=True)),
                        jnp.max(s[2], axis=-1, keepdims=True))
        p = [jnp.exp(sj - m) for sj in s]
        l = (jnp.sum(p[0], axis=-1, keepdims=True) + jnp.sum(p[1], axis=-1, keepdims=True)
             + jnp.sum(p[2], axis=-1, keepdims=True))
        o = (_dot(p[0].astype(BF16), v_refs[0][:, hs]) + _dot(p[1].astype(BF16), v_refs[1][:, hs])
             + _dot(p[2].astype(BF16), v_refs[2][:, hs]))
        o_ref[:, hs] = (o / l).astype(BF16)


def _attention(u3, bias_tile, *, heads, head_dim):
    b, s, _ = u3.shape
    width = heads * head_dim
    nblk = s // ATT_Q
    kern = functools.partial(_attn_kernel, heads=heads, head_dim=head_dim)

    def kv_spec(col, back):
        return pl.BlockSpec((None, ATT_Q, width),
                            lambda bi, i: (bi, jnp.maximum(i - back, 0), col))

    return pl.pallas_call(
        kern,
        grid=(b, nblk),
        in_specs=[
            pl.BlockSpec((None, ATT_Q, width), lambda bi, i: (bi, i, 0)),
            kv_spec(1, 2), kv_spec(1, 1), kv_spec(1, 0),
            kv_spec(2, 2), kv_spec(2, 1), kv_spec(2, 0),
            _const_spec(bias_tile.shape),
        ],
        out_specs=pl.BlockSpec((None, ATT_Q, width), lambda bi, i: (bi, i, 0)),
        out_shape=jax.ShapeDtypeStruct((b, s, width), BF16),
        compiler_params=_params(("parallel", "parallel")),
        name="band_attn",
    )(u3, u3, u3, u3, u3, u3, u3, bias_tile)


def _attn_bias_tile(rel_bias):
    pad = A_LEFT_CHUNKS * CHUNK
    r = jnp.arange(ATT_Q)[:, None]
    c = jnp.arange(ATT_K)[None, :]
    heads = rel_bias.shape[1]
    d_max, d_min = ATT_Q - 1 + pad, pad - (ATT_K - 1)
    rb = rel_bias.astype(F32).T
    g = jnp.concatenate([
        jnp.broadcast_to(rb[:, -1:], (heads, d_max - REL_CLIP)),
        rb[:, ::-1],
        jnp.broadcast_to(rb[:, :1], (heads, -d_min - REL_CLIP + 1)),
    ], axis=1)
    length = g.shape[1]
    skew = jnp.broadcast_to(g[:, None, :], (heads, ATT_Q + 1, length)).reshape(heads, -1)
    skew = skew[:, :ATT_Q * (length + 1)].reshape(heads, ATT_Q, length + 1)
    bias = skew[:, ::-1, :ATT_K]
    qc, kc = r // CHUNK, c // CHUNK
    allowed = (kc >= qc) & (kc <= qc + A_LEFT_CHUNKS)
    return jnp.where(allowed[None], bias, MASK_NEG)


def _log_sigmoid(z):
    return jnp.minimum(z, 0.0) - jnp.log1p(jnp.exp(-jnp.abs(z)))


def _gla_kernel(q_ref, k_ref, v_ref, r_ref, lr_ref, gup_ref, gbias_ref, ogain_ref, tri_ref,
                y_ref, state_ref, *, heads, dk, dv):
    @pl.when(pl.program_id(1) == 0)
    def _():
        state_ref[...] = jnp.zeros_like(state_ref)

    rows = q_ref.shape[0]
    qk_w = heads * dk
    row_id = lax.broadcasted_iota(jnp.int32, (CHUNK, qk_w), 0)
    ci = lax.broadcasted_iota(jnp.int32, (CHUNK, CHUNK), 0)
    cj = lax.broadcasted_iota(jnp.int32, (CHUNK, CHUNK), 1)
    causal = ci >= cj
    n_sub = CHUNK // SUB

    for c in range(rows // CHUNK):
        rs = slice(c * CHUNK, (c + 1) * CHUNK)
        z = jnp.dot(lr_ref[rs, :], gup_ref[...], precision=HIGHEST,
                    preferred_element_type=F32) + gbias_ref[...]
        log_a = _log_sigmoid(z) * (1.0 / B_GATE_TEMP)
        b = jnp.dot(tri_ref[...], log_a, precision=HIGHEST, preferred_element_type=F32)
        q = q_ref[rs, :].astype(F32) * (dk ** -0.5)
        k = k_ref[rs, :].astype(F32)
        b_last = b[CHUNK - 1:CHUNK, :]

        ref_rows = [b[(j + 1) * SUB - 1:(j + 1) * SUB, :] for j in range(n_sub)]
        ref_full = jnp.concatenate(
            [jnp.broadcast_to(rj, (SUB, qk_w)) for rj in ref_rows], axis=0)
        k_dec = k * jnp.exp(ref_full - b)
        q_dec = [q * jnp.exp(jnp.where(row_id >= j * SUB, b - ref_rows[j], -jnp.inf))
                 for j in range(n_sub)]
        k_sub = [jnp.where((row_id >= j * SUB) & (row_id < (j + 1) * SUB), k_dec, 0.0)
                 for j in range(n_sub)]
        q_in = (q * jnp.exp(b)).astype(BF16)
        k_out = (k * jnp.exp(b_last - b)).astype(BF16)
        e_last = jnp.exp(b_last)

        for h in range(heads):
            ks = slice(h * dk, (h + 1) * dk)
            vs = slice(h * dv, (h + 1) * dv)
            q_cat = jnp.concatenate([qd[:, ks] for qd in q_dec], axis=1).astype(BF16)
            k_cat = jnp.concatenate([kz[:, ks] for kz in k_sub], axis=1).astype(BF16)
            attn = jnp.where(causal, _dot_nt(q_cat, k_cat), 0.0)
            v_h = v_ref[rs, vs]
            st = state_ref[h]
            o = _dot(attn.astype(BF16), v_h) + _dot_nt(q_in[:, ks], st.astype(BF16))
            state_ref[h] = st * e_last[:, ks] + _dot_tn(v_h, k_out[:, ks])
            r_h = r_ref[rs, vs].astype(F32)
            y = _rms(o, ogain_ref[:, vs]) * (r_h * jax.nn.sigmoid(r_h))
            y_ref[rs, vs] = y.astype(BF16)


def _gla(u3, lr3, gup_pad, gbias, ogain, tri, *, heads, dk, dv, qk_col, v_col, r_col):
    b, s, _ = u3.shape
    qk_w, v_w = heads * dk, heads * dv
    kern = functools.partial(_gla_kernel, heads=heads, dk=dk, dv=dv)
    return pl.pallas_call(
        kern,
        grid=(b, s // GLA_ROWS),
        in_specs=[
            pl.BlockSpec((None, GLA_ROWS, qk_w), lambda bi, i: (bi, i, qk_col)),
            pl.BlockSpec((None, GLA_ROWS, qk_w), lambda bi, i: (bi, i, qk_col + 1)),
            pl.BlockSpec((None, GLA_ROWS, v_w), lambda bi, i: (bi, i, v_col)),
            pl.BlockSpec((None, GLA_ROWS, v_w), lambda bi, i: (bi, i, r_col)),
            pl.BlockSpec((None, GLA_ROWS, LANES), lambda bi, i: (bi, i, 0)),
            _const_spec(gup_pad.shape),
            _const_spec(gbias.shape),
            _const_spec(ogain.shape),
            _const_spec(tri.shape),
        ],
        out_specs=pl.BlockSpec((None, GLA_ROWS, v_w), lambda bi, i: (bi, i, 0)),
        out_shape=jax.ShapeDtypeStruct((b, s, v_w), BF16),
        scratch_shapes=[pltpu.VMEM((heads, dv, dk), F32)],
        compiler_params=_params(("parallel", "arbitrary")),
        name="gla",
    )(u3, u3, u3, u3, lr3, gup_pad, gbias, ogain, tri)


def _route(logits):
    lane = lax.broadcasted_iota(jnp.int32, logits.shape, 1)
    neg_inf = -jnp.inf
    gl = jnp.where(lane < N_GROUPS, logits, neg_inf)
    gmax = jnp.max(gl, axis=-1, keepdims=True)
    g_idx = jnp.min(jnp.where(gl == gmax, lane, LANES), axis=-1, keepdims=True)
    g_top = 1.0 / jnp.sum(jnp.exp(gl - gmax), axis=-1, keepdims=True)
    e_lane = lane - N_GROUPS
    in_group = (e_lane >= g_idx * EXPERTS_PER_GROUP) & (e_lane < (g_idx + 1) * EXPERTS_PER_GROUP)
    el = jnp.where(in_group, logits, neg_inf)
    e1 = jnp.max(el, axis=-1, keepdims=True)
    i1 = jnp.min(jnp.where(el == e1, lane, LANES), axis=-1, keepdims=True)
    el2 = jnp.where(lane == i1, neg_inf, el)
    e2 = jnp.max(el2, axis=-1, keepdims=True)
    i2 = jnp.min(jnp.where(el2 == e2, lane, LANES), axis=-1, keepdims=True)
    t = jnp.exp(e2 - e1)
    w1 = g_top / (1.0 + t)
    w2 = g_top * t / (1.0 + t)
    return lane, i1, i2, w1, w2


R_E1, R_E2, R_W1, R_W2, R_RANK1, R_RANK2 = range(6)


def _post_kernel(x_ref, ya_ref, yb_ref, ga_ref, gb_ref, pa_ref, pb_ref, wo_ref, gain_ref,
                 wr_ref, br_ref, tri_ref, xmid_ref, h_ref, rinfo_ref, counts_ref):
    @pl.when(pl.program_id(0) == 0)
    def _():
        counts_ref[...] = jnp.zeros_like(counts_ref)

    ga = jax.nn.sigmoid(ga_ref[...].astype(F32))
    gb = jax.nn.sigmoid(gb_ref[...].astype(F32))
    merged = ga * _dot(ya_ref[...], pa_ref[...]) + gb * _dot(yb_ref[...], pb_ref[...])
    x_mid = x_ref[...] + _dot(merged.astype(BF16), wo_ref[...])
    xmid_ref[...] = x_mid
    h = _rms(x_mid, gain_ref[...])
    h_ref[...] = h
    logits = jnp.dot(h, wr_ref[...], precision=HIGHEST, preferred_element_type=F32) + br_ref[...]
    lane, i1, i2, w1, w2 = _route(logits)

    onehot = jnp.where((lane == i1) | (lane == i2), 1.0, 0.0)
    total = _dot(tri_ref[...], onehot.astype(BF16)) + counts_ref[...]
    rank1 = jnp.sum(jnp.where(lane == i1, total, 0.0), axis=-1, keepdims=True)
    rank2 = jnp.sum(jnp.where(lane == i2, total, 0.0), axis=-1, keepdims=True)
    counts_ref[...] += jnp.sum(onehot, axis=0, keepdims=True)

    rec = jnp.zeros(logits.shape, F32)
    for slot, val in ((R_E1, i1.astype(F32)), (R_E2, i2.astype(F32)), (R_W1, w1), (R_W2, w2),
                      (R_RANK1, rank1), (R_RANK2, rank2)):
        rec = jnp.where(lane == slot, val, rec)
    rinfo_ref[...] = rec


def _post(x2d, ya, yb, u2d, pa, pb, wo, gain, wr, br, tri, *, tm, ga_col, gb_col):
    t, d = x2d.shape
    aw, bw = ya.shape[1], yb.shape[1]
    return pl.pallas_call(
        _post_kernel,
        grid=(t // tm,),
        in_specs=[
            pl.BlockSpec((tm, d), lambda i: (i, 0)),
            pl.BlockSpec((tm, aw), lambda i: (i, 0)),
            pl.BlockSpec((tm, bw), lambda i: (i, 0)),
            pl.BlockSpec((tm, d), lambda i: (i, ga_col)),
            pl.BlockSpec((tm, d), lambda i: (i, gb_col)),
            _const_spec(pa.shape), _const_spec(pb.shape), _const_spec(wo.shape),
            _const_spec(gain.shape), _const_spec(wr.shape), _const_spec(br.shape),
            _const_spec(tri.shape),
        ],
        out_specs=[
            pl.BlockSpec((tm, d), lambda i: (i, 0)),
            pl.BlockSpec((tm, d), lambda i: (i, 0)),
            pl.BlockSpec((tm, LANES), lambda i: (i, 0)),
            pl.BlockSpec((1, LANES), lambda i: (0, 0)),
        ],
        out_shape=[
            jax.ShapeDtypeStruct((t, d), F32),
            jax.ShapeDtypeStruct((t, d), F32),
            jax.ShapeDtypeStruct((t, LANES), F32),
            jax.ShapeDtypeStruct((1, LANES), F32),
        ],
        compiler_params=_params(("arbitrary",)),
        name="merge_route",
    )(x2d, ya, yb, u2d, u2d, pa, pb, wo, gain, wr, br, tri)


MOE_TILE = 256
COMBINE_TILE = 256


def _row_copy(src_hbm, row, dst, dst_row, sem):
    return pltpu.make_async_copy(src_hbm.at[pl.ds(row, 1)], dst.at[pl.ds(dst_row, 1)], sem)


def _start_row_copy(copy, index):
    copy.start(priority=index % 2)


def _moe_kernel(tile_expert_ref, src_cur_ref, src_nxt_ref, h_hbm, wg_ref, wu_ref, wd_ref,
                ys_ref, buf_ref, sem_ref, xb_ref):
    del tile_expert_ref
    i = pl.program_id(0)
    n_tiles = pl.num_programs(0)
    slot = i % 2
    rows = ys_ref.shape[0]

    def start_gather(src_ref, s):
        for r in range(rows):
            _start_row_copy(_row_copy(h_hbm, src_ref[0, 0, r], buf_ref.at[s], r, sem_ref.at[s]), r)

    @pl.when(i == 0)
    def _():
        start_gather(src_cur_ref, 0)

    pltpu.make_async_copy(h_hbm.at[pl.ds(0, rows)], buf_ref.at[slot], sem_ref.at[slot]).wait()
    xb_ref[...] = buf_ref[slot].astype(BF16)
    start_gather(src_nxt_ref, 1 - slot)

    x = xb_ref[...]
    a = _dot(x, wg_ref[...].astype(BF16))
    a = a * jax.nn.sigmoid(a) * _dot(x, wu_ref[...].astype(BF16))
    ys_ref[...] = _dot(a.astype(BF16), wd_ref[...].astype(BF16))

    @pl.when(i == n_tiles - 1)
    def _():
        pltpu.make_async_copy(h_hbm.at[pl.ds(0, rows)], buf_ref.at[1 - slot],
                              sem_ref.at[1 - slot]).wait()


def _moe_experts(tile_expert, src3, h, wg, wu, wd):
    n_tiles = src3.shape[0]
    _, d = h.shape
    _, _, de = wg.shape
    grid_spec = pltpu.PrefetchScalarGridSpec(
        num_scalar_prefetch=1,
        grid=(n_tiles,),
        in_specs=[
            pl.BlockSpec((1, 1, MOE_TILE), lambda i, te: (i, 0, 0), memory_space=pltpu.SMEM),
            pl.BlockSpec((1, 1, MOE_TILE), lambda i, te: (jnp.minimum(i + 1, n_tiles - 1), 0, 0),
                         memory_space=pltpu.SMEM),
            pl.BlockSpec(memory_space=pl.ANY),
            pl.BlockSpec((None, d, de), lambda i, te: (te[i], 0, 0)),
            pl.BlockSpec((None, d, de), lambda i, te: (te[i], 0, 0)),
            pl.BlockSpec((None, de, d), lambda i, te: (te[i], 0, 0)),
        ],
        out_specs=pl.BlockSpec((MOE_TILE, d), lambda i, te: (i, 0)),
        scratch_shapes=[pltpu.VMEM((2, MOE_TILE, d), F32), pltpu.SemaphoreType.DMA((2,)),
                        pltpu.VMEM((MOE_TILE, d), BF16)],
    )
    return pl.pallas_call(
        _moe_kernel,
        grid_spec=grid_spec,
        out_shape=jax.ShapeDtypeStruct((n_tiles * MOE_TILE, d), F32),
        compiler_params=_params(("arbitrary",)),
        name="moe_experts",
    )(tile_expert, src3, src3, h, wg, wu, wd)


def _combine_kernel(pos_ref, xmid_ref, rinfo_ref, ys_hbm, o_ref, b1_ref, b2_ref, sem_ref):
    rows = o_ref.shape[0]

    def copies(r):
        return (_row_copy(ys_hbm, pos_ref[0, 0, r], b1_ref, r, sem_ref.at[0]),
                _row_copy(ys_hbm, pos_ref[0, 0, rows + r], b2_ref, r, sem_ref.at[1]))

    for r in range(rows):
        c1, c2 = copies(r)
        _start_row_copy(c1, 0)
        _start_row_copy(c2, 1)
    pltpu.make_async_copy(ys_hbm.at[pl.ds(0, rows)], b1_ref, sem_ref.at[0]).wait()
    pltpu.make_async_copy(ys_hbm.at[pl.ds(0, rows)], b2_ref, sem_ref.at[1]).wait()

    rinfo = rinfo_ref[...]
    w1 = rinfo[:, R_W1:R_W1 + 1]
    w2 = rinfo[:, R_W2:R_W2 + 1]
    o_ref[...] = xmid_ref[...] + w1 * b1_ref[...] + w2 * b2_ref[...]


def _combine(pos3, xmid, rinfo, ys):
    t, d = xmid.shape
    tc = COMBINE_TILE
    return pl.pallas_call(
        _combine_kernel,
        grid=(t // tc,),
        in_specs=[
            pl.BlockSpec((1, 1, 2 * tc), lambda i: (i, 0, 0), memory_space=pltpu.SMEM),
            pl.BlockSpec((tc, d), lambda i: (i, 0)),
            pl.BlockSpec((tc, LANES), lambda i: (i, 0)),
            pl.BlockSpec(memory_space=pl.ANY),
        ],
        out_specs=pl.BlockSpec((tc, d), lambda i: (i, 0)),
        out_shape=jax.ShapeDtypeStruct((t, d), F32),
        scratch_shapes=[pltpu.VMEM((tc, d), F32), pltpu.VMEM((tc, d), F32),
                        pltpu.SemaphoreType.DMA((2,))],
        compiler_params=_params(("arbitrary",)),
        name="moe_combine",
    )(pos3, xmid, rinfo, ys)


def _moe(x_mid, h2, rinfo, counts_row, wg, wu, wd):
    t = x_mid.shape[0]
    ne = wg.shape[0]
    n_tiles = (2 * t) // MOE_TILE + ne
    e_id = rinfo[:, R_E1:R_E2 + 1].astype(jnp.int32) - N_GROUPS
    rank = rinfo[:, R_RANK1:R_RANK2 + 1].astype(jnp.int32)
    counts = counts_row[0, N_GROUPS:N_GROUPS + ne].astype(jnp.int32)
    padded = (counts + MOE_TILE - 1) // MOE_TILE * MOE_TILE
    ends = jnp.cumsum(padded)
    pos = (ends - padded)[e_id] + rank
    token = jnp.broadcast_to(jnp.arange(t, dtype=jnp.int32)[:, None], (t, 2))
    n_slots = n_tiles * MOE_TILE
    src = (jnp.arange(n_slots, dtype=jnp.int32) % t).at[pos.reshape(-1)].set(
        token.reshape(-1), unique_indices=True)
    tile_start = jnp.arange(n_tiles, dtype=jnp.int32) * MOE_TILE
    tile_expert = jnp.minimum(
        jnp.sum((ends[None, :] <= tile_start[:, None]).astype(jnp.int32), axis=1), ne - 1)
    ys = _moe_experts(tile_expert, src.reshape(n_tiles, 1, MOE_TILE), h2, wg, wu, wd)
    tc = COMBINE_TILE
    pos3 = jnp.transpose(pos.reshape(t // tc, tc, 2), (0, 2, 1)).reshape(t // tc, 1, 2 * tc)
    return _combine(pos3, x_mid, rinfo, ys)


def kernel(x, norm_mix_gain, w_in, a_q_gain, a_k_gain, rel_bias, b_gate_up, b_gate_bias,
           b_out_gain, proj_a, proj_b, w_out, norm_ffn_gain, w_group, b_group, w_router,
           b_router, w_gate, w_up, w_down):
    bsz, seq, d = x.shape
    depth = w_in.shape[0]
    t = bsz * seq
    a_width = proj_a.shape[1]
    a_hd = a_width // A_HEADS
    b_vw = proj_b.shape[1]
    b_qkw = b_gate_up.shape[2]
    dk, dv = b_qkw // B_HEADS, b_vw // B_HEADS
    assert seq % ATT_Q == 0 and seq % GLA_ROWS == 0 and d == 2 * a_width == 2 * b_vw
    assert b_qkw * 2 == a_width and N_GROUPS + N_EXPERTS <= LANES

    o_lr = 3 * a_width + 2 * b_qkw + 2 * b_vw
    qk_col = 3 * a_width // b_qkw
    v_col = (3 * a_width + 2 * b_qkw) // b_vw
    r_col = v_col + 1
    ga_col = (3 * a_width + 2 * b_qkw + 2 * b_vw) // d
    gb_col = ga_col + 1
    assert (3 * a_width + 2 * b_qkw + 2 * b_vw) % d == 0

    head_id = jnp.arange(a_width) // a_hd
    ones_bd = (head_id[:, None] == head_id[None, :]).astype(BF16)
    tri = jnp.tril(jnp.ones((CHUNK, CHUNK), F32))
    bias_tile = _attn_bias_tile(rel_bias)

    x2d = x.reshape(t, d)
    tm = 512
    tri_tm = jnp.tril(jnp.ones((tm, tm), BF16), k=-1)
    for l in range(depth):
        w_main = jnp.concatenate([w_in[l][:, :o_lr], w_in[l][:, o_lr + B_GATE_RANK:]],
                                 axis=1).astype(BF16)
        w_lr = jnp.pad(w_in[l][:, o_lr:o_lr + B_GATE_RANK],
                       ((0, 0), (0, LANES - B_GATE_RANK))).astype(BF16)
        u, lr = _inproj(x2d, norm_mix_gain[l][None, :], w_main, w_lr,
                        a_q_gain[l].reshape(1, a_width), a_k_gain[l].reshape(1, a_width),
                        ones_bd, tm=tm, a_width=a_width, head_dim=a_hd)
        u3 = u.reshape(bsz, seq, u.shape[1])
        y_a = _attention(u3, bias_tile, heads=A_HEADS, head_dim=a_hd)
        gup_pad = jnp.pad(b_gate_up[l], ((0, LANES - B_GATE_RANK), (0, 0)))
        y_b = _gla(u3, lr.reshape(bsz, seq, LANES), gup_pad, b_gate_bias[l][None, :],
                   b_out_gain[l].reshape(1, b_vw), tri, heads=B_HEADS, dk=dk, dv=dv,
                   qk_col=qk_col, v_col=v_col, r_col=r_col)
        w_r = jnp.pad(jnp.concatenate([w_group[l], w_router[l]], axis=1),
                      ((0, 0), (0, LANES - N_GROUPS - N_EXPERTS)))
        b_r = jnp.pad(jnp.concatenate([b_group[l], b_router[l]]),
                      (0, LANES - N_GROUPS - N_EXPERTS))[None, :]
        x_mid, h2, rinfo, counts = _post(
            x2d, y_a.reshape(t, a_width), y_b.reshape(t, b_vw), u, proj_a[l].astype(BF16),
            proj_b[l].astype(BF16), w_out[l].astype(BF16), norm_ffn_gain[l][None, :], w_r, b_r,
            tri_tm, tm=tm, ga_col=ga_col, gb_col=gb_col)
        x2d = _moe(x_mid, h2, rinfo, counts, w_gate[l], w_up[l], w_down[l])
    return x2d.reshape(bsz, seq, d)
```

```python
import functools

import jax
import jax.numpy as jnp
from jax import lax
from jax.experimental import pallas as pl
from jax.experimental.pallas import tpu as pltpu

F32 = jnp.float32
BF16 = jnp.bfloat16

EPS = 1e-6
CHUNK = 64
A_HEADS = 8
A_LEFT_CHUNKS = 8
REL_CLIP = 2 * CHUNK
B_HEADS = 4
B_GATE_RANK = 16
B_GATE_TEMP = 16.0
N_GROUPS = 4
EXPERTS_PER_GROUP = 8
N_EXPERTS = N_GROUPS * EXPERTS_PER_GROUP

LANES = 128
LOG2E = 1.4426950408889634
MASK_NEG = -1e30
VMEM_LIMIT = 56 * 1024 * 1024

ATT_Q = 256
ATT_K = ATT_Q + A_LEFT_CHUNKS * CHUNK
GLA_ROWS = 256
SUB = 16


def _params(sem):
    return pltpu.CompilerParams(dimension_semantics=sem, vmem_limit_bytes=VMEM_LIMIT)


def _const_spec(shape):
    nd = len(shape)
    return pl.BlockSpec(shape, lambda *_: (0,) * nd, pipeline_mode=pl.Buffered(1))


def _dot(a, b):
    return jnp.dot(a, b, preferred_element_type=F32)


def _dot_nt(a, b):
    return lax.dot_general(a, b, (((1,), (1,)), ((), ())), preferred_element_type=F32)


def _dot_tn(a, b):
    return lax.dot_general(a, b, (((0,), (0,)), ((), ())), preferred_element_type=F32)


def _split_bf16(w):
    hi = w.astype(BF16)
    return hi, (w - hi.astype(F32)).astype(BF16)


def _dot_split(x, w_hi, w_lo):
    x_hi, x_lo = _split_bf16(x)
    return _dot(x_hi, w_hi) + _dot(x_lo, w_hi) + _dot(x_hi, w_lo)


def _rms(x, gain):
    return x * lax.rsqrt(jnp.mean(x * x, axis=-1, keepdims=True) + EPS) * gain


def _head_norm(acc, ones_bd, gain, head_dim, scale):
    ss = _dot((acc * acc).astype(BF16), ones_bd)
    inv = lax.rsqrt(ss * (1.0 / head_dim) + EPS)
    return acc * inv * (gain * scale)


def _inproj_kernel(x_ref, gain_ref, w_ref, wlr_ref, qg_ref, kg_ref, ones_ref,
                   u_ref, lr_ref, *, a_width, head_dim, col_chunk):
    x = x_ref[...]
    h = _rms(x, gain_ref[...]).astype(BF16)
    n_cols = w_ref.shape[1]
    for c0 in range(0, n_cols, col_chunk):
        acc = _dot(h, w_ref[:, c0:c0 + col_chunk])
        if c0 == 0:
            acc = _head_norm(acc, ones_ref[...], qg_ref[...], head_dim, head_dim ** -0.5 * LOG2E)
        elif c0 == a_width:
            acc = _head_norm(acc, ones_ref[...], kg_ref[...], head_dim, 1.0)
        u_ref[:, c0:c0 + col_chunk] = acc.astype(BF16)
    lr_ref[...] = _dot(h, wlr_ref[...])


def _inproj(x2d, gain, w_main, w_lr, q_gain, k_gain, ones_bd, *, tm, a_width, head_dim):
    t, d = x2d.shape
    n = w_main.shape[1]
    kern = functools.partial(_inproj_kernel, a_width=a_width, head_dim=head_dim,
                             col_chunk=a_width)
    return pl.pallas_call(
        kern,
        grid=(t // tm,),
        in_specs=[
            pl.BlockSpec((tm, d), lambda i: (i, 0)),
            _const_spec((1, d)),
            _const_spec((d, n)),
            _const_spec((d, LANES)),
            _const_spec((1, a_width)),
            _const_spec((1, a_width)),
            _const_spec((a_width, a_width)),
        ],
        out_specs=[
            pl.BlockSpec((tm, n), lambda i: (i, 0)),
            pl.BlockSpec((tm, LANES), lambda i: (i, 0)),
        ],
        out_shape=[
            jax.ShapeDtypeStruct((t, n), BF16),
            jax.ShapeDtypeStruct((t, LANES), F32),
        ],
        compiler_params=_params(("parallel",)),
        name="inproj",
    )(x2d, gain, w_main, w_lr, q_gain, k_gain, ones_bd)


def _attn_kernel(q_ref, k0_ref, k1_ref, k2_ref, v0_ref, v1_ref, v2_ref, bias_ref, o_ref,
                 *, heads, head_dim):
    i = pl.program_id(1)
    k_refs = (k0_ref, k1_ref, k2_ref)
    v_refs = (v0_ref, v1_ref, v2_ref)
    nq = q_ref.shape[0]
    ones = jnp.ones((nq, head_dim), BF16)

    def body(first_valid):
        for h in range(heads):
            hs = slice(h * head_dim, (h + 1) * head_dim)
            q = q_ref[:, hs]
            s = [_dot_nt(q, k_refs[j][:, hs]) + bias_ref[h, :, j * nq:(j + 1) * nq]
                 for j in range(first_valid, 3)]
            m = jnp.max(s[0], axis=-1, keepdims=True)
            for sj in s[1:]:
                m = jnp.maximum(m, jnp.max(sj, axis=-1, keepdims=True))
            o = None
            for j, sj in zip(range(first_valid, 3), s):
                p = jnp.exp2(sj - m).astype(BF16)
                v_aug = jnp.concatenate([v_refs[j][:, hs], ones], axis=1)
                oj = _dot(p, v_aug)
                o = oj if o is None else o + oj
            o_ref[:, hs] = (o[:, :head_dim] / o[:, head_dim:head_dim + 1]).astype(BF16)

    for fv in range(3):
        @pl.when(jnp.maximum(2 - i, 0) == fv)
        def _(fv=fv):
            body(fv)


def _attention(u3, bias_tile, *, heads, head_dim):
    b, s, _ = u3.shape
    width = heads * head_dim
    nblk = s // ATT_Q
    kern = functools.partial(_attn_kernel, heads=heads, head_dim=head_dim)

    def kv_spec(col, back):
        return pl.BlockSpec((None, ATT_Q, width),
                            lambda bi, i: (bi, jnp.maximum(i - back, 0), col))

    return pl.pallas_call(
        kern,
        grid=(b, nblk),
        in_specs=[
            pl.BlockSpec((None, ATT_Q, width), lambda bi, i: (bi, i, 0)),
            kv_spec(1, 2), kv_spec(1, 1), kv_spec(1, 0),
            kv_spec(2, 2), kv_spec(2, 1), kv_spec(2, 0),
            _const_spec(bias_tile.shape),
        ],
        out_specs=pl.BlockSpec((None, ATT_Q, width), lambda bi, i: (bi, i, 0)),
        out_shape=jax.ShapeDtypeStruct((b, s, width), BF16),
        compiler_params=_params(("parallel", "parallel")),
        name="band_attn",
    )(u3, u3, u3, u3, u3, u3, u3, bias_tile)


def _attn_bias_tile(rel_bias):
    pad = A_LEFT_CHUNKS * CHUNK
    r = jnp.arange(ATT_Q)[:, None]
    c = jnp.arange(ATT_K)[None, :]
    heads = rel_bias.shape[1]
    d_max, d_min = ATT_Q - 1 + pad, pad - (ATT_K - 1)
    rb = rel_bias.astype(F32).T
    g = jnp.concatenate([
        jnp.broadcast_to(rb[:, -1:], (heads, d_max - REL_CLIP)),
        rb[:, ::-1],
        jnp.broadcast_to(rb[:, :1], (heads, -d_min - REL_CLIP + 1)),
    ], axis=1)
    length = g.shape[1]
    skew = jnp.broadcast_to(g[:, None, :], (heads, ATT_Q + 1, length)).reshape(heads, -1)
    skew = skew[:, :ATT_Q * (length + 1)].reshape(heads, ATT_Q, length + 1)
    bias = skew[:, ::-1, :ATT_K]
    qc, kc = r // CHUNK, c // CHUNK
    allowed = (kc >= qc) & (kc <= qc + A_LEFT_CHUNKS)
    return jnp.where(allowed[None], bias * LOG2E, MASK_NEG)


def _log_sigmoid(z):
    return jnp.minimum(z, 0.0) - jnp.log(1.0 + jnp.exp(-jnp.abs(z)))


def _gla_kernel(q_ref, k_ref, v_ref, r_ref, lr_ref, guph_ref, gupl_ref, gbias_ref, ogain_ref, tri_ref,
                y_ref, state_ref, *, heads, dk, dv):
    @pl.when(pl.program_id(1) == 0)
    def _():
        state_ref[...] = jnp.zeros_like(state_ref)

    rows = q_ref.shape[0]
    qk_w = heads * dk
    row_id = lax.broadcasted_iota(jnp.int32, (CHUNK, qk_w), 0)
    ci = lax.broadcasted_iota(jnp.int32, (CHUNK, CHUNK), 0)
    cj = lax.broadcasted_iota(jnp.int32, (CHUNK, CHUNK), 1)
    causal = ci >= cj
    n_sub = CHUNK // SUB

    for c in range(rows // CHUNK):
        rs = slice(c * CHUNK, (c + 1) * CHUNK)
        z = _dot_split(lr_ref[rs, :], guph_ref[...], gupl_ref[...]) + gbias_ref[...]
        log_a = _log_sigmoid(z) * (1.0 / B_GATE_TEMP)
        la_hi = log_a.astype(BF16)
        la_r1 = log_a - la_hi.astype(F32)
        la_mid = la_r1.astype(BF16)
        la_lo = (la_r1 - la_mid.astype(F32)).astype(BF16)
        b = _dot(tri_ref[...], la_hi) + _dot(tri_ref[...], la_mid) + _dot(tri_ref[...], la_lo)
        q = q_ref[rs, :].astype(F32) * (dk ** -0.5)
        k = k_ref[rs, :].astype(F32)
        b_last = b[CHUNK - 1:CHUNK, :]

        ref_rows = [b[(j + 1) * SUB - 1:(j + 1) * SUB, :] for j in range(n_sub)]
        ref_full = jnp.concatenate(
            [jnp.broadcast_to(rj, (SUB, qk_w)) for rj in ref_rows], axis=0)
        k_dec = k * jnp.exp(ref_full - b)
        q_dec = [q * jnp.exp(jnp.where(row_id >= j * SUB, b - ref_rows[j], -jnp.inf))
                 for j in range(n_sub)]
        k_sub = [jnp.where((row_id >= j * SUB) & (row_id < (j + 1) * SUB), k_dec, 0.0)
                 for j in range(n_sub)]
        q_in = (q * jnp.exp(b)).astype(BF16)
        k_out = (k * jnp.exp(b_last - b)).astype(BF16)
        e_last = jnp.exp(b_last)

        for h in range(heads):
            ks = slice(h * dk, (h + 1) * dk)
            vs = slice(h * dv, (h + 1) * dv)
            q_cat = jnp.concatenate([qd[:, ks] for qd in q_dec], axis=1).astype(BF16)
            k_cat = jnp.concatenate([kz[:, ks] for kz in k_sub], axis=1).astype(BF16)
            attn = jnp.where(causal, _dot_nt(q_cat, k_cat), 0.0)
            v_h = v_ref[rs, vs]
            st = state_ref[h]
            o = _dot(attn.astype(BF16), v_h) + _dot_nt(q_in[:, ks], st.astype(BF16))
            state_ref[h] = st * e_last[:, ks] + _dot_tn(v_h, k_out[:, ks])
            r_h = r_ref[rs, vs].astype(F32)
            y = _rms(o, ogain_ref[:, vs]) * (r_h * jax.nn.sigmoid(r_h))
            y_ref[rs, vs] = y.astype(BF16)


def _gla(u3, lr3, gup_pad, gbias, ogain, tri, *, heads, dk, dv, qk_col, v_col, r_col):
    b, s, _ = u3.shape
    qk_w, v_w = heads * dk, heads * dv
    kern = functools.partial(_gla_kernel, heads=heads, dk=dk, dv=dv)
    return pl.pallas_call(
        kern,
        grid=(b, s // GLA_ROWS),
        in_specs=[
            pl.BlockSpec((None, GLA_ROWS, qk_w), lambda bi, i: (bi, i, qk_col)),
            pl.BlockSpec((None, GLA_ROWS, qk_w), lambda bi, i: (bi, i, qk_col + 1)),
            pl.BlockSpec((None, GLA_ROWS, v_w), lambda bi, i: (bi, i, v_col)),
            pl.BlockSpec((None, GLA_ROWS, v_w), lambda bi, i: (bi, i, r_col)),
            pl.BlockSpec((None, GLA_ROWS, LANES), lambda bi, i: (bi, i, 0)),
            _const_spec(gup_pad.shape),
            _const_spec(gup_pad.shape),
            _const_spec(gbias.shape),
            _const_spec(ogain.shape),
            _const_spec(tri.shape),
        ],
        out_specs=pl.BlockSpec((None, GLA_ROWS, v_w), lambda bi, i: (bi, i, 0)),
        out_shape=jax.ShapeDtypeStruct((b, s, v_w), BF16),
        scratch_shapes=[pltpu.VMEM((heads, dv, dk), F32)],
        compiler_params=_params(("parallel", "arbitrary")),
        name="gla",
    )(u3, u3, u3, u3, lr3, *_split_bf16(gup_pad), gbias, ogain, tri)


def _route(logits):
    lane = lax.broadcasted_iota(jnp.int32, logits.shape, 1)
    neg_inf = -jnp.inf
    gl = jnp.where(lane < N_GROUPS, logits, neg_inf)
    gmax = jnp.max(gl, axis=-1, keepdims=True)
    g_idx = jnp.min(jnp.where(gl == gmax, lane, LANES), axis=-1, keepdims=True)
    g_top = 1.0 / jnp.sum(jnp.exp(gl - gmax), axis=-1, keepdims=True)
    e_lane = lane - N_GROUPS
    in_group = (e_lane >= g_idx * EXPERTS_PER_GROUP) & (e_lane < (g_idx + 1) * EXPERTS_PER_GROUP)
    el = jnp.where(in_group, logits, neg_inf)
    e1 = jnp.max(el, axis=-1, keepdims=True)
    i1 = jnp.min(jnp.where(el == e1, lane, LANES), axis=-1, keepdims=True)
    el2 = jnp.where(lane == i1, neg_inf, el)
    e2 = jnp.max(el2, axis=-1, keepdims=True)
    i2 = jnp.min(jnp.where(el2 == e2, lane, LANES), axis=-1, keepdims=True)
    t = jnp.exp(e2 - e1)
    w1 = g_top / (1.0 + t)
    w2 = g_top * t / (1.0 + t)
    return lane, i1, i2, w1, w2


R_E1, R_E2, R_W1, R_W2, R_RANK1, R_RANK2 = range(6)


def _post_kernel(x_ref, ya_ref, yb_ref, ga_ref, gb_ref, pa_ref, pb_ref, wo_ref, gain_ref,
                 wrh_ref, wrl_ref, br_ref, tri_ref, xmid_ref, h_ref, rinfo_ref, counts_ref):
    @pl.when(pl.program_id(0) == 0)
    def _():
        counts_ref[...] = jnp.zeros_like(counts_ref)

    ga = jax.nn.sigmoid(ga_ref[...].astype(F32))
    gb = jax.nn.sigmoid(gb_ref[...].astype(F32))
    merged = ga * _dot(ya_ref[...], pa_ref[...]) + gb * _dot(yb_ref[...], pb_ref[...])
    x_mid = x_ref[...] + _dot(merged.astype(BF16), wo_ref[...])
    xmid_ref[...] = x_mid
    h = _rms(x_mid, gain_ref[...])
    h_ref[...] = h
    logits = _dot_split(h, wrh_ref[...], wrl_ref[...]) + br_ref[...]
    lane, i1, i2, w1, w2 = _route(logits)

    onehot = jnp.where((lane == i1) | (lane == i2), 1.0, 0.0)
    total = _dot(tri_ref[...], onehot.astype(BF16)) + counts_ref[...]
    rank1 = jnp.sum(jnp.where(lane == i1, total, 0.0), axis=-1, keepdims=True)
    rank2 = jnp.sum(jnp.where(lane == i2, total, 0.0), axis=-1, keepdims=True)
    counts_ref[...] += jnp.sum(onehot, axis=0, keepdims=True)

    rec = jnp.zeros(logits.shape, F32)
    for slot, val in ((R_E1, i1.astype(F32)), (R_E2, i2.astype(F32)), (R_W1, w1), (R_W2, w2),
                      (R_RANK1, rank1), (R_RANK2, rank2)):
        rec = jnp.where(lane == slot, val, rec)
    rinfo_ref[...] = rec


def _post(x2d, ya, yb, u2d, pa, pb, wo, gain, wr, br, *, tm, ga_col, gb_col):
    t, d = x2d.shape
    aw, bw = ya.shape[1], yb.shape[1]
    wr_hi, wr_lo = _split_bf16(wr)
    tri = jnp.tril(jnp.ones((tm, tm), BF16), k=-1)
    return pl.pallas_call(
        _post_kernel,
        grid=(t // tm,),
        in_specs=[
            pl.BlockSpec((tm, d), lambda i: (i, 0)),
            pl.BlockSpec((tm, aw), lambda i: (i, 0)),
            pl.BlockSpec((tm, bw), lambda i: (i, 0)),
            pl.BlockSpec((tm, d), lambda i: (i, ga_col)),
            pl.BlockSpec((tm, d), lambda i: (i, gb_col)),
            _const_spec(pa.shape), _const_spec(pb.shape), _const_spec(wo.shape),
            _const_spec(gain.shape), _const_spec(wr.shape), _const_spec(wr.shape),
            _const_spec(br.shape), _const_spec(tri.shape),
        ],
        out_specs=[
            pl.BlockSpec((tm, d), lambda i: (i, 0)),
            pl.BlockSpec((tm, d), lambda i: (i, 0)),
            pl.BlockSpec((tm, LANES), lambda i: (i, 0)),
            pl.BlockSpec((1, LANES), lambda i: (0, 0)),
        ],
        out_shape=[
            jax.ShapeDtypeStruct((t, d), F32),
            jax.ShapeDtypeStruct((t, d), F32),
            jax.ShapeDtypeStruct((t, LANES), F32),
            jax.ShapeDtypeStruct((1, LANES), F32),
        ],
        compiler_params=_params(("arbitrary",)),
        name="merge_route",
    )(x2d, ya, yb, u2d, u2d, pa, pb, wo, gain, wr_hi, wr_lo, br, tri)


MOE_TILE = 256
COMBINE_TILE = 256


def _row_copy(src_hbm, row, dst, dst_row, sem):
    return pltpu.make_async_copy(src_hbm.at[pl.ds(row, 1)], dst.at[pl.ds(dst_row, 1)], sem)


def _start_row_copy(copy, index):
    copy.start(priority=index % 2)


def _moe_kernel(tile_expert_ref, src_cur_ref, src_nxt_ref, h_hbm, wg_ref, wu_ref, wd_ref,
                ys_ref, buf_ref, sem_ref, xb_ref):
    del tile_expert_ref
    i = pl.program_id(0)
    n_tiles = pl.num_programs(0)
    slot = i % 2
    rows = ys_ref.shape[0]

    def start_gather(src_ref, s):
        for r in range(rows):
            _start_row_copy(_row_copy(h_hbm, src_ref[0, 0, r], buf_ref.at[s], r, sem_ref.at[s]), r)

    @pl.when(i == 0)
    def _():
        start_gather(src_cur_ref, 0)

    pltpu.make_async_copy(h_hbm.at[pl.ds(0, rows)], buf_ref.at[slot], sem_ref.at[slot]).wait()
    xb_ref[...] = buf_ref[slot].astype(BF16)
    start_gather(src_nxt_ref, 1 - slot)

    x = xb_ref[...]
    a = _dot(x, wg_ref[...].astype(BF16))
    a = a * jax.nn.sigmoid(a) * _dot(x, wu_ref[...].astype(BF16))
    ys_ref[...] = _dot(a.astype(BF16), wd_ref[...].astype(BF16))

    @pl.when(i == n_tiles - 1)
    def _():
        pltpu.make_async_copy(h_hbm.at[pl.ds(0, rows)], buf_ref.at[1 - slot],
                              sem_ref.at[1 - slot]).wait()


def _moe_experts(tile_expert, src3, h, wg, wu, wd):
    n_tiles = src3.shape[0]
    _, d = h.shape
    _, _, de = wg.shape
    grid_spec = pltpu.PrefetchScalarGridSpec(
        num_scalar_prefetch=1,
        grid=(n_tiles,),
        in_specs=[
            pl.BlockSpec((1, 1, MOE_TILE), lambda i, te: (i, 0, 0), memory_space=pltpu.SMEM),
            pl.BlockSpec((1, 1, MOE_TILE), lambda i, te: (jnp.minimum(i + 1, n_tiles - 1), 0, 0),
                         memory_space=pltpu.SMEM),
            pl.BlockSpec(memory_space=pl.ANY),
            pl.BlockSpec((None, d, de), lambda i, te: (te[i], 0, 0)),
            pl.BlockSpec((None, d, de), lambda i, te: (te[i], 0, 0)),
            pl.BlockSpec((None, de, d), lambda i, te: (te[i], 0, 0)),
        ],
        out_specs=pl.BlockSpec((MOE_TILE, d), lambda i, te: (i, 0)),
        scratch_shapes=[pltpu.VMEM((2, MOE_TILE, d), F32), pltpu.SemaphoreType.DMA((2,)),
                        pltpu.VMEM((MOE_TILE, d), BF16)],
    )
    return pl.pallas_call(
        _moe_kernel,
        grid_spec=grid_spec,
        out_shape=jax.ShapeDtypeStruct((n_tiles * MOE_TILE, d), F32),
        compiler_params=_params(("arbitrary",)),
        name="moe_experts",
    )(tile_expert, src3, src3, h, wg, wu, wd)


def _combine_kernel(pos_ref, xmid_ref, rinfo_ref, ys_hbm, o_ref, b1_ref, b2_ref, sem_ref):
    rows = o_ref.shape[0]

    def copies(r):
        return (_row_copy(ys_hbm, pos_ref[0, 0, r], b1_ref, r, sem_ref.at[0]),
                _row_copy(ys_hbm, pos_ref[0, 0, rows + r], b2_ref, r, sem_ref.at[1]))

    for r in range(rows):
        c1, c2 = copies(r)
        _start_row_copy(c1, 0)
        _start_row_copy(c2, 1)
    pltpu.make_async_copy(ys_hbm.at[pl.ds(0, rows)], b1_ref, sem_ref.at[0]).wait()
    pltpu.make_async_copy(ys_hbm.at[pl.ds(0, rows)], b2_ref, sem_ref.at[1]).wait()

    rinfo = rinfo_ref[...]
    w1 = rinfo[:, R_W1:R_W1 + 1]
    w2 = rinfo[:, R_W2:R_W2 + 1]
    o_ref[...] = xmid_ref[...] + w1 * b1_ref[...] + w2 * b2_ref[...]


def _combine(pos3, xmid, rinfo, ys):
    t, d = xmid.shape
    tc = COMBINE_TILE
    return pl.pallas_call(
        _combine_kernel,
        grid=(t // tc,),
        in_specs=[
            pl.BlockSpec((1, 1, 2 * tc), lambda i: (i, 0, 0), memory_space=pltpu.SMEM),
            pl.BlockSpec((tc, d), lambda i: (i, 0)),
            pl.BlockSpec((tc, LANES), lambda i: (i, 0)),
            pl.BlockSpec(memory_space=pl.ANY),
        ],
        out_specs=pl.BlockSpec((tc, d), lambda i: (i, 0)),
        out_shape=jax.ShapeDtypeStruct((t, d), F32),
        scratch_shapes=[pltpu.VMEM((tc, d), F32), pltpu.VMEM((tc, d), F32),
                        pltpu.SemaphoreType.DMA((2,))],
        compiler_params=_params(("arbitrary",)),
        name="moe_combine",
    )(pos3, xmid, rinfo, ys)


def _moe(x_mid, h2, rinfo, counts_row, wg, wu, wd):
    t = x_mid.shape[0]
    ne = wg.shape[0]
    n_tiles = (2 * t) // MOE_TILE + ne
    e_id = rinfo[:, R_E1:R_E2 + 1].astype(jnp.int32) - N_GROUPS
    rank = rinfo[:, R_RANK1:R_RANK2 + 1].astype(jnp.int32)
    counts = counts_row[0, N_GROUPS:N_GROUPS + ne].astype(jnp.int32)
    padded = (counts + MOE_TILE - 1) // MOE_TILE * MOE_TILE
    ends = jnp.cumsum(padded)
    pos = (ends - padded)[e_id] + rank
    token = jnp.broadcast_to(jnp.arange(t, dtype=jnp.int32)[:, None], (t, 2))
    n_slots = n_tiles * MOE_TILE
    src = (jnp.arange(n_slots, dtype=jnp.int32) % t).at[pos.reshape(-1)].set(
        token.reshape(-1), unique_indices=True)
    tile_start = jnp.arange(n_tiles, dtype=jnp.int32) * MOE_TILE
    tile_expert = jnp.minimum(
        jnp.sum((ends[None, :] <= tile_start[:, None]).astype(jnp.int32), axis=1), ne - 1)
    ys = _moe_experts(tile_expert, src.reshape(n_tiles, 1, MOE_TILE), h2, wg, wu, wd)
    tc = COMBINE_TILE
    pos3 = jnp.transpose(pos.reshape(t // tc, tc, 2), (0, 2, 1)).reshape(t // tc, 1, 2 * tc)
    return _combine(pos3, x_mid, rinfo, ys)


def kernel(x, norm_mix_gain, w_in, a_q_gain, a_k_gain, rel_bias, b_gate_up, b_gate_bias,
           b_out_gain, proj_a, proj_b, w_out, norm_ffn_gain, w_group, b_group, w_router,
           b_router, w_gate, w_up, w_down):
    bsz, seq, d = x.shape
    depth = w_in.shape[0]
    t = bsz * seq
    a_width = proj_a.shape[1]
    a_hd = a_width // A_HEADS
    b_vw = proj_b.shape[1]
    b_qkw = b_gate_up.shape[2]
    dk, dv = b_qkw // B_HEADS, b_vw // B_HEADS
    assert seq % ATT_Q == 0 and seq % GLA_ROWS == 0 and d == 2 * a_width == 2 * b_vw
    assert b_qkw * 2 == a_width and N_GROUPS + N_EXPERTS <= LANES

    o_lr = 3 * a_width + 2 * b_qkw + 2 * b_vw
    qk_col = 3 * a_width // b_qkw
    v_col = (3 * a_width + 2 * b_qkw) // b_vw
    r_col = v_col + 1
    ga_col = (3 * a_width + 2 * b_qkw + 2 * b_vw) // d
    gb_col = ga_col + 1
    assert (3 * a_width + 2 * b_qkw + 2 * b_vw) % d == 0

    head_id = jnp.arange(a_width) // a_hd
    ones_bd = (head_id[:, None] == head_id[None, :]).astype(BF16)
    tri = jnp.tril(jnp.ones((CHUNK, CHUNK), BF16))
    bias_tile = _attn_bias_tile(rel_bias)

    x2d = x.reshape(t, d)
    tm = 512
    for l in range(depth):
        w_main = jnp.concatenate([w_in[l][:, :o_lr], w_in[l][:, o_lr + B_GATE_RANK:]],
                                 axis=1).astype(BF16)
        w_lr = jnp.pad(w_in[l][:, o_lr:o_lr + B_GATE_RANK],
                       ((0, 0), (0, LANES - B_GATE_RANK))).astype(BF16)
        u, lr = _inproj(x2d, norm_mix_gain[l][None, :], w_main, w_lr,
                        a_q_gain[l].reshape(1, a_width), a_k_gain[l].reshape(1, a_width),
                        ones_bd, tm=tm, a_width=a_width, head_dim=a_hd)
        u3 = u.reshape(bsz, seq, u.shape[1])
        y_a = _attention(u3, bias_tile, heads=A_HEADS, head_dim=a_hd)
        gup_pad = jnp.pad(b_gate_up[l], ((0, LANES - B_GATE_RANK), (0, 0)))
        y_b = _gla(u3, lr.reshape(bsz, seq, LANES), gup_pad, b_gate_bias[l][None, :],
                   b_out_gain[l].reshape(1, b_vw), tri, heads=B_HEADS, dk=dk, dv=dv,
                   qk_col=qk_col, v_col=v_col, r_col=r_col)
        w_r = jnp.pad(jnp.concatenate([w_group[l], w_router[l]], axis=1),
                      ((0, 0), (0, LANES - N_GROUPS - N_EXPERTS)))
        b_r = jnp.pad(jnp.concatenate([b_group[l], b_router[l]]),
                      (0, LANES - N_GROUPS - N_EXPERTS))[None, :]
        x_mid, h2, rinfo, counts = _post(
            x2d, y_a.reshape(t, a_width), y_b.reshape(t, b_vw), u, proj_a[l].astype(BF16),
            proj_b[l].astype(BF16), w_out[l].astype(BF16), norm_ffn_gain[l][None, :], w_r, b_r,
            tm=tm, ga_col=ga_col, gb_col=gb_col)
        x2d = _moe(x_mid, h2, rinfo, counts, w_gate[l], w_up[l], w_down[l])
    return x2d.reshape(bsz, seq, d)
```

```python
import functools

import jax
import jax.numpy as jnp
from jax import lax
from jax.experimental import pallas as pl
from jax.experimental.pallas import tpu as pltpu

F32 = jnp.float32
BF16 = jnp.bfloat16

EPS = 1e-6
CHUNK = 64
A_HEADS = 8
A_LEFT_CHUNKS = 8
REL_CLIP = 2 * CHUNK
B_HEADS = 4
B_GATE_RANK = 16
B_GATE_TEMP = 16.0
N_GROUPS = 4
EXPERTS_PER_GROUP = 8
N_EXPERTS = N_GROUPS * EXPERTS_PER_GROUP

LANES = 128
LOG2E = 1.4426950408889634
MASK_NEG = -1e30
VMEM_LIMIT = 56 * 1024 * 1024

ATT_Q = 256
ATT_K = ATT_Q + A_LEFT_CHUNKS * CHUNK
GLA_ROWS = 256
SUB = 16


def _params(sem):
    return pltpu.CompilerParams(dimension_semantics=sem, vmem_limit_bytes=VMEM_LIMIT)


def _const_spec(shape):
    nd = len(shape)
    return pl.BlockSpec(shape, lambda *_: (0,) * nd, pipeline_mode=pl.Buffered(1))


def _dot(a, b):
    return jnp.dot(a, b, preferred_element_type=F32)


def _dot_nt(a, b):
    return lax.dot_general(a, b, (((1,), (1,)), ((), ())), preferred_element_type=F32)


def _dot_tn(a, b):
    return lax.dot_general(a, b, (((0,), (0,)), ((), ())), preferred_element_type=F32)


def _split_bf16(w):
    hi = w.astype(BF16)
    return hi, (w - hi.astype(F32)).astype(BF16)


def _dot_split(x, w_hi, w_lo):
    x_hi, x_lo = _split_bf16(x)
    return _dot(x_hi, w_hi) + _dot(x_lo, w_hi) + _dot(x_hi, w_lo)


def _rms(x, gain):
    return x * lax.rsqrt(jnp.mean(x * x, axis=-1, keepdims=True) + EPS) * gain


def _head_norm(acc, ones_bd, gain, head_dim, scale):
    ss = _dot((acc * acc).astype(BF16), ones_bd)
    inv = lax.rsqrt(ss * (1.0 / head_dim) + EPS)
    return acc * inv * (gain * scale)


def _inproj_kernel(x_ref, gain_ref, w_ref, wlr_ref, qg_ref, kg_ref, ones_ref,
                   u_ref, lr_ref, *, a_width, head_dim, col_chunk):
    x = x_ref[...]
    h = _rms(x, gain_ref[...]).astype(BF16)
    n_cols = w_ref.shape[1]
    for c0 in range(0, n_cols, col_chunk):
        acc = _dot(h, w_ref[:, c0:c0 + col_chunk])
        if c0 == 0:
            acc = _head_norm(acc, ones_ref[...], qg_ref[...], head_dim, head_dim ** -0.5 * LOG2E)
        elif c0 == a_width:
            acc = _head_norm(acc, ones_ref[...], kg_ref[...], head_dim, 1.0)
        u_ref[:, c0:c0 + col_chunk] = acc.astype(BF16)
    lr_ref[...] = _dot(h, wlr_ref[...])


def _inproj(x2d, gain, w_main, w_lr, q_gain, k_gain, ones_bd, *, tm, a_width, head_dim):
    t, d = x2d.shape
    n = w_main.shape[1]
    kern = functools.partial(_inproj_kernel, a_width=a_width, head_dim=head_dim,
                             col_chunk=a_width)
    return pl.pallas_call(
        kern,
        grid=(t // tm,),
        in_specs=[
            pl.BlockSpec((tm, d), lambda i: (i, 0)),
            _const_spec((1, d)),
            _const_spec((d, n)),
            _const_spec((d, LANES)),
            _const_spec((1, a_width)),
            _const_spec((1, a_width)),
            _const_spec((a_width, a_width)),
        ],
        out_specs=[
            pl.BlockSpec((tm, n), lambda i: (i, 0)),
            pl.BlockSpec((tm, LANES), lambda i: (i, 0)),
        ],
        out_shape=[
            jax.ShapeDtypeStruct((t, n), BF16),
            jax.ShapeDtypeStruct((t, LANES), F32),
        ],
        compiler_params=_params(("parallel",)),
        name="inproj",
    )(x2d, gain, w_main, w_lr, q_gain, k_gain, ones_bd)


def _attn_kernel(q_ref, k0_ref, k1_ref, k2_ref, v0_ref, v1_ref, v2_ref, bias_ref, o_ref,
                 *, heads, head_dim):
    i = pl.program_id(1)
    k_refs = (k0_ref, k1_ref, k2_ref)
    v_refs = (v0_ref, v1_ref, v2_ref)
    nq = q_ref.shape[0]
    ones = jnp.ones((nq, head_dim), BF16)

    def body(first_valid):
        for h in range(heads):
            hs = slice(h * head_dim, (h + 1) * head_dim)
            q = q_ref[:, hs]
            s = [_dot_nt(q, k_refs[j][:, hs]) + bias_ref[h, :, j * nq:(j + 1) * nq]
                 for j in range(first_valid, 3)]
            m = jnp.max(s[0], axis=-1, keepdims=True)
            for sj in s[1:]:
                m = jnp.maximum(m, jnp.max(sj, axis=-1, keepdims=True))
            o = None
            for j, sj in zip(range(first_valid, 3), s):
                p = jnp.exp2(sj - m).astype(BF16)
                v_aug = jnp.concatenate([v_refs[j][:, hs], ones], axis=1)
                oj = _dot(p, v_aug)
                o = oj if o is None else o + oj
            o_ref[:, hs] = (o[:, :head_dim] / o[:, head_dim:head_dim + 1]).astype(BF16)

    for fv in range(3):
        @pl.when(jnp.maximum(2 - i, 0) == fv)
        def _(fv=fv):
            body(fv)


def _attention(u3, bias_tile, *, heads, head_dim):
    b, s, _ = u3.shape
    width = heads * head_dim
    nblk = s // ATT_Q
    kern = functools.partial(_attn_kernel, heads=heads, head_dim=head_dim)

    def kv_spec(col, back):
        return pl.BlockSpec((None, ATT_Q, width),
                            lambda bi, i: (bi, jnp.maximum(i - back, 0), col))

    return pl.pallas_call(
        kern,
        grid=(b, nblk),
        in_specs=[
            pl.BlockSpec((None, ATT_Q, width), lambda bi, i: (bi, i, 0)),
            kv_spec(1, 2), kv_spec(1, 1), kv_spec(1, 0),
            kv_spec(2, 2), kv_spec(2, 1), kv_spec(2, 0),
            _const_spec(bias_tile.shape),
        ],
        out_specs=pl.BlockSpec((None, ATT_Q, width), lambda bi, i: (bi, i, 0)),
        out_shape=jax.ShapeDtypeStruct((b, s, width), BF16),
        compiler_params=_params(("parallel", "parallel")),
        name="band_attn",
    )(u3, u3, u3, u3, u3, u3, u3, bias_tile)


def _attn_bias_tile(rel_bias):
    pad = A_LEFT_CHUNKS * CHUNK
    r = jnp.arange(ATT_Q)[:, None]
    c = jnp.arange(ATT_K)[None, :]
    heads = rel_bias.shape[1]
    d_max, d_min = ATT_Q - 1 + pad, pad - (ATT_K - 1)
    rb = rel_bias.astype(F32).T
    g = jnp.concatenate([
        jnp.broadcast_to(rb[:, -1:], (heads, d_max - REL_CLIP)),
        rb[:, ::-1],
        jnp.broadcast_to(rb[:, :1], (heads, -d_min - REL_CLIP + 1)),
    ], axis=1)
    length = g.shape[1]
    skew = jnp.broadcast_to(g[:, None, :], (heads, ATT_Q + 1, length)).reshape(heads, -1)
    skew = skew[:, :ATT_Q * (length + 1)].reshape(heads, ATT_Q, length + 1)
    bias = skew[:, ::-1, :ATT_K]
    qc, kc = r // CHUNK, c // CHUNK
    allowed = (kc >= qc) & (kc <= qc + A_LEFT_CHUNKS)
    return jnp.where(allowed[None], bias * LOG2E, MASK_NEG)


def _log_sigmoid(z):
    return jnp.minimum(z, 0.0) - jnp.log(1.0 + jnp.exp(-jnp.abs(z)))


def _gla_kernel(q_ref, k_ref, v_ref, r_ref, lr_ref, guph_ref, gupl_ref, gbias_ref, ogain_ref, tri_ref,
                y_ref, state_ref, *, heads, dk, dv):
    @pl.when(pl.program_id(1) == 0)
    def _():
        state_ref[...] = jnp.zeros_like(state_ref)

    rows = q_ref.shape[0]
    qk_w = heads * dk
    row_id = lax.broadcasted_iota(jnp.int32, (CHUNK, qk_w), 0)
    ci = lax.broadcasted_iota(jnp.int32, (CHUNK, CHUNK), 0)
    cj = lax.broadcasted_iota(jnp.int32, (CHUNK, CHUNK), 1)
    causal = ci >= cj
    n_sub = CHUNK // SUB

    for c in range(rows // CHUNK):
        rs = slice(c * CHUNK, (c + 1) * CHUNK)
        z = _dot_split(lr_ref[rs, :], guph_ref[...], gupl_ref[...]) + gbias_ref[...]
        log_a = _log_sigmoid(z) * (1.0 / B_GATE_TEMP)
        la_hi = log_a.astype(BF16)
        la_r1 = log_a - la_hi.astype(F32)
        la_mid = la_r1.astype(BF16)
        la_lo = (la_r1 - la_mid.astype(F32)).astype(BF16)
        b = _dot(tri_ref[...], la_hi) + _dot(tri_ref[...], la_mid) + _dot(tri_ref[...], la_lo)
        q = q_ref[rs, :].astype(F32) * (dk ** -0.5)
        k = k_ref[rs, :].astype(F32)
        b_last = b[CHUNK - 1:CHUNK, :]

        ref_rows = [b[(j + 1) * SUB - 1:(j + 1) * SUB, :] for j in range(n_sub)]
        ref_full = jnp.concatenate(
            [jnp.broadcast_to(rj, (SUB, qk_w)) for rj in ref_rows], axis=0)
        k_dec = k * jnp.exp(ref_full - b)
        q_dec = [q * jnp.exp(jnp.where(row_id >= j * SUB, b - ref_rows[j], -jnp.inf))
                 for j in range(n_sub)]
        k_sub = [jnp.where((row_id >= j * SUB) & (row_id < (j + 1) * SUB), k_dec, 0.0)
                 for j in range(n_sub)]
        q_in = (q * jnp.exp(b)).astype(BF16)
        k_out = (k * jnp.exp(b_last - b)).astype(BF16)
        e_last = jnp.exp(b_last)

        for h in range(heads):
            ks = slice(h * dk, (h + 1) * dk)
            vs = slice(h * dv, (h + 1) * dv)
            q_cat = jnp.concatenate([qd[:, ks] for qd in q_dec], axis=1).astype(BF16)
            k_cat = jnp.concatenate([kz[:, ks] for kz in k_sub], axis=1).astype(BF16)
            attn = jnp.where(causal, _dot_nt(q_cat, k_cat), 0.0)
            v_h = v_ref[rs, vs]
            st = state_ref[h]
            o = _dot(attn.astype(BF16), v_h) + _dot_nt(q_in[:, ks], st.astype(BF16))
            state_ref[h] = st * e_last[:, ks] + _dot_tn(v_h, k_out[:, ks])
            r_h = r_ref[rs, vs].astype(F32)
            y = _rms(o, ogain_ref[:, vs]) * (r_h * jax.nn.sigmoid(r_h))
            y_ref[rs, vs] = y.astype(BF16)


def _gla(u3, lr3, gup_pad, gbias, ogain, tri, *, heads, dk, dv, qk_col, v_col, r_col):
    b, s, _ = u3.shape
    qk_w, v_w = heads * dk, heads * dv
    kern = functools.partial(_gla_kernel, heads=heads, dk=dk, dv=dv)
    return pl.pallas_call(
        kern,
        grid=(b, s // GLA_ROWS),
        in_specs=[
            pl.BlockSpec((None, GLA_ROWS, qk_w), lambda bi, i: (bi, i, qk_col)),
            pl.BlockSpec((None, GLA_ROWS, qk_w), lambda bi, i: (bi, i, qk_col + 1)),
            pl.BlockSpec((None, GLA_ROWS, v_w), lambda bi, i: (bi, i, v_col)),
            pl.BlockSpec((None, GLA_ROWS, v_w), lambda bi, i: (bi, i, r_col)),
            pl.BlockSpec((None, GLA_ROWS, LANES), lambda bi, i: (bi, i, 0)),
            _const_spec(gup_pad.shape),
            _const_spec(gup_pad.shape),
            _const_spec(gbias.shape),
            _const_spec(ogain.shape),
            _const_spec(tri.shape),
        ],
        out_specs=pl.BlockSpec((None, GLA_ROWS, v_w), lambda bi, i: (bi, i, 0)),
        out_shape=jax.ShapeDtypeStruct((b, s, v_w), BF16),
        scratch_shapes=[pltpu.VMEM((heads, dv, dk), F32)],
        compiler_params=_params(("parallel", "arbitrary")),
        name="gla",
    )(u3, u3, u3, u3, lr3, *_split_bf16(gup_pad), gbias, ogain, tri)


def _route(logits):
    lane = lax.broadcasted_iota(jnp.int32, logits.shape, 1)
    neg_inf = -jnp.inf
    gl = jnp.where(lane < N_GROUPS, logits, neg_inf)
    gmax = jnp.max(gl, axis=-1, keepdims=True)
    g_idx = jnp.min(jnp.where(gl == gmax, lane, LANES), axis=-1, keepdims=True)
    g_top = 1.0 / jnp.sum(jnp.exp(gl - gmax), axis=-1, keepdims=True)
    e_lane = lane - N_GROUPS
    in_group = (e_lane >= g_idx * EXPERTS_PER_GROUP) & (e_lane < (g_idx + 1) * EXPERTS_PER_GROUP)
    el = jnp.where(in_group, logits, neg_inf)
    e1 = jnp.max(el, axis=-1, keepdims=True)
    i1 = jnp.min(jnp.where(el == e1, lane, LANES), axis=-1, keepdims=True)
    el2 = jnp.where(lane == i1, neg_inf, el)
    e2 = jnp.max(el2, axis=-1, keepdims=True)
    i2 = jnp.min(jnp.where(el2 == e2, lane, LANES), axis=-1, keepdims=True)
    t = jnp.exp(e2 - e1)
    w1 = g_top / (1.0 + t)
    w2 = g_top * t / (1.0 + t)
    return lane, i1, i2, w1, w2


R_E1, R_E2, R_W1, R_W2, R_RANK1, R_RANK2 = range(6)


def _row_copy(src, row, dst, dst_row, sem):
    return pltpu.make_async_copy(src.at[pl.ds(row, 1)], dst.at[pl.ds(dst_row, 1)], sem)


def _post_kernel(x_ref, ya_ref, yb_ref, ga_ref, gb_ref, pa_ref, pb_ref, wo_ref, gain_ref,
                 wrh_ref, wrl_ref, br_ref, tri_ref, xmid_ref, rinfo_ref, counts_ref, xs_hbm,
                 h_scr, idx_vmem, idx_smem, row_sem, idx_sem, *, capacity, dump_row):
    i = pl.program_id(0)
    n_steps = pl.num_programs(0)
    slot = i % 2
    rows = x_ref.shape[0]

    def dispatch(s):
        for r in range(rows):
            for c in range(2):
                _row_copy(h_scr.at[s], r, xs_hbm, idx_smem[s, c, r], row_sem.at[c]).start(priority=c)

    def wait_dispatch(s):
        for c in range(2):
            pltpu.make_async_copy(h_scr.at[s], xs_hbm.at[pl.ds(0, rows)], row_sem.at[c]).wait()

    @pl.when(i == 0)
    def _():
        counts_ref[...] = jnp.zeros_like(counts_ref)
        h_scr[1] = jnp.zeros(h_scr.shape[1:], F32)

        def fill(r, carry):
            idx_smem[1, 0, r] = dump_row + r
            idx_smem[1, 1, r] = dump_row + rows + r
            return carry
        lax.fori_loop(0, rows, fill, 0)

    dispatch(1 - slot)

    ga = jax.nn.sigmoid(ga_ref[...].astype(F32))
    gb = jax.nn.sigmoid(gb_ref[...].astype(F32))
    merged = ga * _dot(ya_ref[...], pa_ref[...]) + gb * _dot(yb_ref[...], pb_ref[...])
    x_mid = x_ref[...] + _dot(merged.astype(BF16), wo_ref[...])
    xmid_ref[...] = x_mid
    h = _rms(x_mid, gain_ref[...])
    h_scr[slot] = h
    logits = _dot_split(h, wrh_ref[...], wrl_ref[...]) + br_ref[...]
    lane, i1, i2, w1, w2 = _route(logits)

    onehot = jnp.where((lane == i1) | (lane == i2), 1.0, 0.0)
    total = _dot(tri_ref[...], onehot.astype(BF16)) + counts_ref[...]
    rank1 = jnp.sum(jnp.where(lane == i1, total, 0.0), axis=-1, keepdims=True)
    rank2 = jnp.sum(jnp.where(lane == i2, total, 0.0), axis=-1, keepdims=True)
    counts_ref[...] += jnp.sum(onehot, axis=0, keepdims=True)

    rec = jnp.zeros(logits.shape, F32)
    for lane_id, val in ((R_E1, i1.astype(F32)), (R_E2, i2.astype(F32)), (R_W1, w1), (R_W2, w2),
                         (R_RANK1, rank1), (R_RANK2, rank2)):
        rec = jnp.where(lane == lane_id, val, rec)
    rinfo_ref[...] = rec

    dest1 = (i1 - N_GROUPS) * capacity + rank1.astype(jnp.int32)
    dest2 = (i2 - N_GROUPS) * capacity + rank2.astype(jnp.int32)
    dest = jnp.where(lane == 0, dest1, jnp.where(lane == 1, dest2, 0))
    idx_vmem[...] = jnp.transpose(dest, (1, 0))[0:idx_vmem.shape[0], :]
    idx_copy = pltpu.make_async_copy(idx_vmem, idx_smem.at[slot], idx_sem)
    idx_copy.start()
    idx_copy.wait()

    wait_dispatch(1 - slot)

    @pl.when(i == n_steps - 1)
    def _():
        dispatch(slot)
        wait_dispatch(slot)


def _post(x2d, ya, yb, u2d, pa, pb, wo, gain, wr, br, *, tm, n_experts, ga_col, gb_col):
    t, d = x2d.shape
    aw, bw = ya.shape[1], yb.shape[1]
    wr_hi, wr_lo = _split_bf16(wr)
    tri = jnp.tril(jnp.ones((tm, tm), BF16), k=-1)
    capacity = t
    dump_row = n_experts * capacity
    sublanes = 8
    kern = functools.partial(_post_kernel, capacity=capacity, dump_row=dump_row)
    return pl.pallas_call(
        kern,
        grid=(t // tm,),
        in_specs=[
            pl.BlockSpec((tm, d), lambda i: (i, 0)),
            pl.BlockSpec((tm, aw), lambda i: (i, 0)),
            pl.BlockSpec((tm, bw), lambda i: (i, 0)),
            pl.BlockSpec((tm, d), lambda i: (i, ga_col)),
            pl.BlockSpec((tm, d), lambda i: (i, gb_col)),
            _const_spec(pa.shape), _const_spec(pb.shape), _const_spec(wo.shape),
            _const_spec(gain.shape), _const_spec(wr.shape), _const_spec(wr.shape),
            _const_spec(br.shape), _const_spec(tri.shape),
        ],
        out_specs=[
            pl.BlockSpec((tm, d), lambda i: (i, 0)),
            pl.BlockSpec((tm, LANES), lambda i: (i, 0)),
            pl.BlockSpec((1, LANES), lambda i: (0, 0)),
            pl.BlockSpec(memory_space=pl.ANY),
        ],
        out_shape=[
            jax.ShapeDtypeStruct((t, d), F32),
            jax.ShapeDtypeStruct((t, LANES), F32),
            jax.ShapeDtypeStruct((1, LANES), F32),
            jax.ShapeDtypeStruct((dump_row + 2 * tm, d), F32),
        ],
        scratch_shapes=[
            pltpu.VMEM((2, tm, d), F32),
            pltpu.VMEM((sublanes, tm), jnp.int32),
            pltpu.SMEM((2, sublanes, tm), jnp.int32),
            pltpu.SemaphoreType.DMA((2,)),
            pltpu.SemaphoreType.DMA(()),
        ],
        compiler_params=_params(("arbitrary",)),
        name="merge_route",
    )(x2d, ya, yb, u2d, u2d, pa, pb, wo, gain, wr_hi, wr_lo, br, tri)


MOE_TILE = 256
COMBINE_TILE = 256


def _start_row_copy(copy, index):
    copy.start(priority=index % 2)


def _moe_kernel(tile_block_ref, tile_expert_ref, tile_valid_ref, x_ref, wg_ref, wu_ref, wd_ref,
                ys_ref):
    del tile_block_ref, tile_expert_ref
    valid = tile_valid_ref[pl.program_id(0)]
    row = lax.broadcasted_iota(jnp.int32, (x_ref.shape[0], 1), 0)
    x = jnp.where(row < valid, x_ref[...], 0.0).astype(BF16)
    a = _dot(x, wg_ref[...].astype(BF16))
    a = a * jax.nn.sigmoid(a) * _dot(x, wu_ref[...].astype(BF16))
    ys_ref[...] = _dot(a.astype(BF16), wd_ref[...].astype(BF16))


def _moe_experts(tile_block, tile_expert, tile_valid, xs, wg, wu, wd):
    n_tiles = tile_block.shape[0]
    _, d = xs.shape
    _, _, de = wg.shape
    grid_spec = pltpu.PrefetchScalarGridSpec(
        num_scalar_prefetch=3,
        grid=(n_tiles,),
        in_specs=[
            pl.BlockSpec((MOE_TILE, d), lambda i, tb, te, tv: (tb[i], 0)),
            pl.BlockSpec((None, d, de), lambda i, tb, te, tv: (te[i], 0, 0)),
            pl.BlockSpec((None, d, de), lambda i, tb, te, tv: (te[i], 0, 0)),
            pl.BlockSpec((None, de, d), lambda i, tb, te, tv: (te[i], 0, 0)),
        ],
        out_specs=pl.BlockSpec((MOE_TILE, d), lambda i, tb, te, tv: (i, 0)),
    )
    return pl.pallas_call(
        _moe_kernel,
        grid_spec=grid_spec,
        out_shape=jax.ShapeDtypeStruct((n_tiles * MOE_TILE, d), F32),
        compiler_params=_params(("arbitrary",)),
        name="moe_experts",
    )(tile_block, tile_expert, tile_valid, xs, wg, wu, wd)


def _combine_kernel(pos_ref, xmid_ref, rinfo_ref, ys_hbm, o_ref, b1_ref, b2_ref, sem_ref):
    rows = o_ref.shape[0]

    def copies(r):
        return (_row_copy(ys_hbm, pos_ref[0, 0, r], b1_ref, r, sem_ref.at[0]),
                _row_copy(ys_hbm, pos_ref[0, 0, rows + r], b2_ref, r, sem_ref.at[1]))

    for r in range(rows):
        c1, c2 = copies(r)
        _start_row_copy(c1, 0)
        _start_row_copy(c2, 1)
    pltpu.make_async_copy(ys_hbm.at[pl.ds(0, rows)], b1_ref, sem_ref.at[0]).wait()
    pltpu.make_async_copy(ys_hbm.at[pl.ds(0, rows)], b2_ref, sem_ref.at[1]).wait()

    rinfo = rinfo_ref[...]
    w1 = rinfo[:, R_W1:R_W1 + 1]
    w2 = rinfo[:, R_W2:R_W2 + 1]
    o_ref[...] = xmid_ref[...] + w1 * b1_ref[...] + w2 * b2_ref[...]


def _combine(pos3, xmid, rinfo, ys):
    t, d = xmid.shape
    tc = COMBINE_TILE
    return pl.pallas_call(
        _combine_kernel,
        grid=(t // tc,),
        in_specs=[
            pl.BlockSpec((1, 1, 2 * tc), lambda i: (i, 0, 0), memory_space=pltpu.SMEM),
            pl.BlockSpec((tc, d), lambda i: (i, 0)),
            pl.BlockSpec((tc, LANES), lambda i: (i, 0)),
            pl.BlockSpec(memory_space=pl.ANY),
        ],
        out_specs=pl.BlockSpec((tc, d), lambda i: (i, 0)),
        out_shape=jax.ShapeDtypeStruct((t, d), F32),
        scratch_shapes=[pltpu.VMEM((tc, d), F32), pltpu.VMEM((tc, d), F32),
                        pltpu.SemaphoreType.DMA((2,))],
        compiler_params=_params(("arbitrary",)),
        name="moe_combine",
    )(pos3, xmid, rinfo, ys)


def _moe(x_mid, xs, rinfo, counts_row, wg, wu, wd):
    t = x_mid.shape[0]
    ne = wg.shape[0]
    capacity = t
    assert capacity % MOE_TILE == 0
    n_tiles = (2 * t) // MOE_TILE + ne
    e_id = rinfo[:, R_E1:R_E2 + 1].astype(jnp.int32) - N_GROUPS
    rank = rinfo[:, R_RANK1:R_RANK2 + 1].astype(jnp.int32)
    counts = counts_row[0, N_GROUPS:N_GROUPS + ne].astype(jnp.int32)
    tiles_per = (counts + MOE_TILE - 1) // MOE_TILE
    tile_end = jnp.cumsum(tiles_per)
    tile_first = tile_end - tiles_per
    tile = jnp.arange(n_tiles, dtype=jnp.int32)
    tile_expert = jnp.minimum(
        jnp.sum((tile_end[None, :] <= tile[:, None]).astype(jnp.int32), axis=1), ne - 1)
    local = tile - tile_first[tile_expert]
    used = tile < tile_end[-1]
    tile_block = jnp.where(used, tile_expert * (capacity // MOE_TILE) + local, 0)
    tile_valid = jnp.where(used, jnp.clip(counts[tile_expert] - local * MOE_TILE, 0, MOE_TILE), 0)
    ys = _moe_experts(tile_block, tile_expert, tile_valid, xs, wg, wu, wd)
    pos = (tile_first * MOE_TILE)[e_id] + rank
    tc = COMBINE_TILE
    pos3 = jnp.transpose(pos.reshape(t // tc, tc, 2), (0, 2, 1)).reshape(t // tc, 1, 2 * tc)
    return _combine(pos3, x_mid, rinfo, ys)


def kernel(x, norm_mix_gain, w_in, a_q_gain, a_k_gain, rel_bias, b_gate_up, b_gate_bias,
           b_out_gain, proj_a, proj_b, w_out, norm_ffn_gain, w_group, b_group, w_router,
           b_router, w_gate, w_up, w_down):
    bsz, seq, d = x.shape
    depth = w_in.shape[0]
    t = bsz * seq
    a_width = proj_a.shape[1]
    a_hd = a_width // A_HEADS
    b_vw = proj_b.shape[1]
    b_qkw = b_gate_up.shape[2]
    dk, dv = b_qkw // B_HEADS, b_vw // B_HEADS
    assert seq % ATT_Q == 0 and seq % GLA_ROWS == 0 and d == 2 * a_width == 2 * b_vw
    assert b_qkw * 2 == a_width and N_GROUPS + N_EXPERTS <= LANES

    o_lr = 3 * a_width + 2 * b_qkw + 2 * b_vw
    qk_col = 3 * a_width // b_qkw
    v_col = (3 * a_width + 2 * b_qkw) // b_vw
    r_col = v_col + 1
    ga_col = (3 * a_width + 2 * b_qkw + 2 * b_vw) // d
    gb_col = ga_col + 1
    assert (3 * a_width + 2 * b_qkw + 2 * b_vw) % d == 0

    head_id = jnp.arange(a_width) // a_hd
    ones_bd = (head_id[:, None] == head_id[None, :]).astype(BF16)
    tri = jnp.tril(jnp.ones((CHUNK, CHUNK), BF16))
    bias_tile = _attn_bias_tile(rel_bias)

    x2d = x.reshape(t, d)
    tm = 512
    for l in range(depth):
        w_main = jnp.concatenate([w_in[l][:, :o_lr], w_in[l][:, o_lr + B_GATE_RANK:]],
                                 axis=1).astype(BF16)
        w_lr = jnp.pad(w_in[l][:, o_lr:o_lr + B_GATE_RANK],
                       ((0, 0), (0, LANES - B_GATE_RANK))).astype(BF16)
        u, lr = _inproj(x2d, norm_mix_gain[l][None, :], w_main, w_lr,
                        a_q_gain[l].reshape(1, a_width), a_k_gain[l].reshape(1, a_width),
                        ones_bd, tm=tm, a_width=a_width, head_dim=a_hd)
        u3 = u.reshape(bsz, seq, u.shape[1])
        y_a = _attention(u3, bias_tile, heads=A_HEADS, head_dim=a_hd)
        gup_pad = jnp.pad(b_gate_up[l], ((0, LANES - B_GATE_RANK), (0, 0)))
        y_b = _gla(u3, lr.reshape(bsz, seq, LANES), gup_pad, b_gate_bias[l][None, :],
                   b_out_gain[l].reshape(1, b_vw), tri, heads=B_HEADS, dk=dk, dv=dv,
                   qk_col=qk_col, v_col=v_col, r_col=r_col)
        w_r = jnp.pad(jnp.concatenate([w_group[l], w_router[l]], axis=1),
                      ((0, 0), (0, LANES - N_GROUPS - N_EXPERTS)))
        b_r = jnp.pad(jnp.concatenate([b_group[l], b_router[l]]),
                      (0, LANES - N_GROUPS - N_EXPERTS))[None, :]
        x_mid, rinfo, counts, xs = _post(
            x2d, y_a.reshape(t, a_width), y_b.reshape(t, b_vw), u, proj_a[l].astype(BF16),
            proj_b[l].astype(BF16), w_out[l].astype(BF16), norm_ffn_gain[l][None, :], w_r, b_r,
            tm=tm, n_experts=N_EXPERTS, ga_col=ga_col, gb_col=gb_col)
        x2d = _moe(x_mid, xs, rinfo, counts, w_gate[l], w_up[l], w_down[l])
    return x2d.reshape(bsz, seq, d)
```

```python
import functools

import jax
import jax.numpy as jnp
from jax import lax
from jax.experimental import pallas as pl
from jax.experimental.pallas import tpu as pltpu

F32 = jnp.float32
BF16 = jnp.bfloat16

EPS = 1e-6
CHUNK = 64
A_HEADS = 8
A_LEFT_CHUNKS = 8
REL_CLIP = 2 * CHUNK
B_HEADS = 4
B_GATE_RANK = 16
B_GATE_TEMP = 16.0
N_GROUPS = 4
EXPERTS_PER_GROUP = 8
N_EXPERTS = N_GROUPS * EXPERTS_PER_GROUP

LANES = 128
LOG2E = 1.4426950408889634
MASK_NEG = -1e30
VMEM_LIMIT = 56 * 1024 * 1024

ATT_Q = 256
ATT_K = ATT_Q + A_LEFT_CHUNKS * CHUNK
GLA_ROWS = 256
SUB = 16


def _params(sem):
    return pltpu.CompilerParams(dimension_semantics=sem, vmem_limit_bytes=VMEM_LIMIT)


def _const_spec(shape):
    nd = len(shape)
    return pl.BlockSpec(shape, lambda *_: (0,) * nd, pipeline_mode=pl.Buffered(1))


def _dot(a, b):
    return jnp.dot(a, b, preferred_element_type=F32)


def _dot_nt(a, b):
    return lax.dot_general(a, b, (((1,), (1,)), ((), ())), preferred_element_type=F32)


def _dot_tn(a, b):
    return lax.dot_general(a, b, (((0,), (0,)), ((), ())), preferred_element_type=F32)


def _split_bf16(w):
    hi = w.astype(BF16)
    return hi, (w - hi.astype(F32)).astype(BF16)


def _dot_split(x, w_hi, w_lo):
    x_hi, x_lo = _split_bf16(x)
    return _dot(x_hi, w_hi) + _dot(x_lo, w_hi) + _dot(x_hi, w_lo)


def _rms(x, gain):
    return x * lax.rsqrt(jnp.mean(x * x, axis=-1, keepdims=True) + EPS) * gain


def _head_norm(acc, ones_bd, gain, head_dim, scale):
    ss = _dot((acc * acc).astype(BF16), ones_bd)
    inv = lax.rsqrt(ss * (1.0 / head_dim) + EPS)
    return acc * inv * (gain * scale)


def _inproj_kernel(x_ref, gain_ref, w_ref, wlr_ref, qg_ref, kg_ref, ones_ref,
                   u_ref, lr_ref, *, a_width, head_dim, col_chunk):
    x = x_ref[...]
    h = _rms(x, gain_ref[...]).astype(BF16)
    n_cols = w_ref.shape[1]
    for c0 in range(0, n_cols, col_chunk):
        acc = _dot(h, w_ref[:, c0:c0 + col_chunk])
        if c0 == 0:
            acc = _head_norm(acc, ones_ref[...], qg_ref[...], head_dim, head_dim ** -0.5 * LOG2E)
        elif c0 == a_width:
            acc = _head_norm(acc, ones_ref[...], kg_ref[...], head_dim, 1.0)
        u_ref[:, c0:c0 + col_chunk] = acc.astype(BF16)
    lr_ref[...] = _dot(h, wlr_ref[...])


def _inproj(x2d, gain, w_main, w_lr, q_gain, k_gain, ones_bd, *, tm, a_width, head_dim):
    t, d = x2d.shape
    n = w_main.shape[1]
    kern = functools.partial(_inproj_kernel, a_width=a_width, head_dim=head_dim,
                             col_chunk=a_width)
    return pl.pallas_call(
        kern,
        grid=(t // tm,),
        in_specs=[
            pl.BlockSpec((tm, d), lambda i: (i, 0)),
            _const_spec((1, d)),
            _const_spec((d, n)),
            _const_spec((d, LANES)),
            _const_spec((1, a_width)),
            _const_spec((1, a_width)),
            _const_spec((a_width, a_width)),
        ],
        out_specs=[
            pl.BlockSpec((tm, n), lambda i: (i, 0)),
            pl.BlockSpec((tm, LANES), lambda i: (i, 0)),
        ],
        out_shape=[
            jax.ShapeDtypeStruct((t, n), BF16),
            jax.ShapeDtypeStruct((t, LANES), F32),
        ],
        compiler_params=_params(("parallel",)),
        name="inproj",
    )(x2d, gain, w_main, w_lr, q_gain, k_gain, ones_bd)


def _attn_kernel(q_ref, k0_ref, k1_ref, k2_ref, v0_ref, v1_ref, v2_ref, bias_ref, o_ref,
                 *, heads, head_dim):
    i = pl.program_id(1)
    k_refs = (k0_ref, k1_ref, k2_ref)
    v_refs = (v0_ref, v1_ref, v2_ref)
    nq = q_ref.shape[0]
    ones = jnp.ones((nq, head_dim), BF16)

    def body(first_valid):
        for h in range(heads):
            hs = slice(h * head_dim, (h + 1) * head_dim)
            q = q_ref[:, hs]
            s = [_dot_nt(q, k_refs[j][:, hs]) + bias_ref[h, :, j * nq:(j + 1) * nq]
                 for j in range(first_valid, 3)]
            m = jnp.max(s[0], axis=-1, keepdims=True)
            for sj in s[1:]:
                m = jnp.maximum(m, jnp.max(sj, axis=-1, keepdims=True))
            o = None
            for j, sj in zip(range(first_valid, 3), s):
                p = jnp.exp2(sj - m).astype(BF16)
                v_aug = jnp.concatenate([v_refs[j][:, hs], ones], axis=1)
                oj = _dot(p, v_aug)
                o = oj if o is None else o + oj
            o_ref[:, hs] = (o[:, :head_dim] / o[:, head_dim:head_dim + 1]).astype(BF16)

    for fv in range(3):
        @pl.when(jnp.maximum(2 - i, 0) == fv)
        def _(fv=fv):
            body(fv)


def _attention(u3, bias_tile, *, heads, head_dim):
    b, s, _ = u3.shape
    width = heads * head_dim
    nblk = s // ATT_Q
    kern = functools.partial(_attn_kernel, heads=heads, head_dim=head_dim)

    def kv_spec(col, back):
        return pl.BlockSpec((None, ATT_Q, width),
                            lambda bi, i: (bi, jnp.maximum(i - back, 0), col))

    return pl.pallas_call(
        kern,
        grid=(b, nblk),
        in_specs=[
            pl.BlockSpec((None, ATT_Q, width), lambda bi, i: (bi, i, 0)),
            kv_spec(1, 2), kv_spec(1, 1), kv_spec(1, 0),
            kv_spec(2, 2), kv_spec(2, 1), kv_spec(2, 0),
            _const_spec(bias_tile.shape),
        ],
        out_specs=pl.BlockSpec((None, ATT_Q, width), lambda bi, i: (bi, i, 0)),
        out_shape=jax.ShapeDtypeStruct((b, s, width), BF16),
        compiler_params=_params(("parallel", "parallel")),
        name="band_attn",
    )(u3, u3, u3, u3, u3, u3, u3, bias_tile)


def _attn_bias_tile(rel_bias):
    pad = A_LEFT_CHUNKS * CHUNK
    r = jnp.arange(ATT_Q)[:, None]
    c = jnp.arange(ATT_K)[None, :]
    heads = rel_bias.shape[1]
    rb = rel_bias.astype(F32).T
    hi, lo = rb[:, -1:], rb[:, :1]
    g = jnp.concatenate([
        jnp.broadcast_to(hi, (heads, pad - REL_CLIP + 1)),
        rb[:, 2 * REL_CLIP - 1:0:-1],
        jnp.broadcast_to(lo, (heads, ATT_K - pad - REL_CLIP)),
        lo,
        jnp.broadcast_to(hi, (heads, ATT_Q - 1)),
    ], axis=1)
    length = ATT_Q + ATT_K
    skew = jnp.broadcast_to(g[:, None, :], (heads, ATT_Q, length)).reshape(heads, -1)
    skew = skew[:, :ATT_Q * (length - 1)].reshape(heads, ATT_Q, length - 1)
    bias = skew[:, :, :ATT_K]
    qc, kc = r // CHUNK, c // CHUNK
    allowed = (kc >= qc) & (kc <= qc + A_LEFT_CHUNKS)
    return jnp.where(allowed[None], bias * LOG2E, MASK_NEG)


def _log_sigmoid(z):
    return jnp.minimum(z, 0.0) - jnp.log(1.0 + jnp.exp(-jnp.abs(z)))


def _gla_kernel(q_ref, k_ref, v_ref, r_ref, lr_ref, guph_ref, gupl_ref, gbias_ref, ogain_ref, tri_ref,
                y_ref, state_ref, *, heads, dk, dv):
    @pl.when(pl.program_id(1) == 0)
    def _():
        state_ref[...] = jnp.zeros_like(state_ref)

    rows = q_ref.shape[0]
    qk_w = heads * dk
    row_id = lax.broadcasted_iota(jnp.int32, (CHUNK, qk_w), 0)
    ci = lax.broadcasted_iota(jnp.int32, (CHUNK, CHUNK), 0)
    cj = lax.broadcasted_iota(jnp.int32, (CHUNK, CHUNK), 1)
    causal = ci >= cj
    n_sub = CHUNK // SUB

    for c in range(rows // CHUNK):
        rs = slice(c * CHUNK, (c + 1) * CHUNK)
        z = _dot_split(lr_ref[rs, :], guph_ref[...], gupl_ref[...]) + gbias_ref[...]
        log_a = _log_sigmoid(z) * (1.0 / B_GATE_TEMP)
        la_hi = log_a.astype(BF16)
        la_r1 = log_a - la_hi.astype(F32)
        la_mid = la_r1.astype(BF16)
        la_lo = (la_r1 - la_mid.astype(F32)).astype(BF16)
        b = _dot(tri_ref[...], la_hi) + _dot(tri_ref[...], la_mid) + _dot(tri_ref[...], la_lo)
        q = q_ref[rs, :].astype(F32) * (dk ** -0.5)
        k = k_ref[rs, :].astype(F32)
        b_last = b[CHUNK - 1:CHUNK, :]

        ref_rows = [b[(j + 1) * SUB - 1:(j + 1) * SUB, :] for j in range(n_sub)]
        ref_full = jnp.concatenate(
            [jnp.broadcast_to(rj, (SUB, qk_w)) for rj in ref_rows], axis=0)
        k_dec = k * jnp.exp(ref_full - b)
        q_dec = [q * jnp.exp(jnp.where(row_id >= j * SUB, b - ref_rows[j], -jnp.inf))
                 for j in range(n_sub)]
        k_sub = [jnp.where((row_id >= j * SUB) & (row_id < (j + 1) * SUB), k_dec, 0.0)
                 for j in range(n_sub)]
        q_in = (q * jnp.exp(b)).astype(BF16)
        k_out = (k * jnp.exp(b_last - b)).astype(BF16)
        e_last = jnp.exp(b_last)

        for h in range(heads):
            ks = slice(h * dk, (h + 1) * dk)
            vs = slice(h * dv, (h + 1) * dv)
            q_cat = jnp.concatenate([qd[:, ks] for qd in q_dec], axis=1).astype(BF16)
            k_cat = jnp.concatenate([kz[:, ks] for kz in k_sub], axis=1).astype(BF16)
            attn = jnp.where(causal, _dot_nt(q_cat, k_cat), 0.0)
            v_h = v_ref[rs, vs]
            st = state_ref[h]
            o = _dot(attn.astype(BF16), v_h) + _dot_nt(q_in[:, ks], st.astype(BF16))
            state_ref[h] = st * e_last[:, ks] + _dot_tn(v_h, k_out[:, ks])
            r_h = r_ref[rs, vs].astype(F32)
            y = _rms(o, ogain_ref[:, vs]) * (r_h * jax.nn.sigmoid(r_h))
            y_ref[rs, vs] = y.astype(BF16)


def _gla(u3, lr3, gup_pad, gbias, ogain, tri, *, heads, dk, dv, qk_col, v_col, r_col):
    b, s, _ = u3.shape
    qk_w, v_w = heads * dk, heads * dv
    kern = functools.partial(_gla_kernel, heads=heads, dk=dk, dv=dv)
    return pl.pallas_call(
        kern,
        grid=(b, s // GLA_ROWS),
        in_specs=[
            pl.BlockSpec((None, GLA_ROWS, qk_w), lambda bi, i: (bi, i, qk_col)),
            pl.BlockSpec((None, GLA_ROWS, qk_w), lambda bi, i: (bi, i, qk_col + 1)),
            pl.BlockSpec((None, GLA_ROWS, v_w), lambda bi, i: (bi, i, v_col)),
            pl.BlockSpec((None, GLA_ROWS, v_w), lambda bi, i: (bi, i, r_col)),
            pl.BlockSpec((None, GLA_ROWS, LANES), lambda bi, i: (bi, i, 0)),
            _const_spec(gup_pad.shape),
            _const_spec(gup_pad.shape),
            _const_spec(gbias.shape),
            _const_spec(ogain.shape),
            _const_spec(tri.shape),
        ],
        out_specs=pl.BlockSpec((None, GLA_ROWS, v_w), lambda bi, i: (bi, i, 0)),
        out_shape=jax.ShapeDtypeStruct((b, s, v_w), BF16),
        scratch_shapes=[pltpu.VMEM((heads, dv, dk), F32)],
        compiler_params=_params(("parallel", "arbitrary")),
        name="gla",
    )(u3, u3, u3, u3, lr3, *_split_bf16(gup_pad), gbias, ogain, tri)


def _route(logits):
    lane = lax.broadcasted_iota(jnp.int32, logits.shape, 1)
    neg_inf = -jnp.inf
    gl = jnp.where(lane < N_GROUPS, logits, neg_inf)
    gmax = jnp.max(gl, axis=-1, keepdims=True)
    g_idx = jnp.min(jnp.where(gl == gmax, lane, LANES), axis=-1, keepdims=True)
    g_top = 1.0 / jnp.sum(jnp.exp(gl - gmax), axis=-1, keepdims=True)
    e_lane = lane - N_GROUPS
    in_group = (e_lane >= g_idx * EXPERTS_PER_GROUP) & (e_lane < (g_idx + 1) * EXPERTS_PER_GROUP)
    el = jnp.where(in_group, logits, neg_inf)
    e1 = jnp.max(el, axis=-1, keepdims=True)
    i1 = jnp.min(jnp.where(el == e1, lane, LANES), axis=-1, keepdims=True)
    el2 = jnp.where(lane == i1, neg_inf, el)
    e2 = jnp.max(el2, axis=-1, keepdims=True)
    i2 = jnp.min(jnp.where(el2 == e2, lane, LANES), axis=-1, keepdims=True)
    t = jnp.exp(e2 - e1)
    w1 = g_top / (1.0 + t)
    w2 = g_top * t / (1.0 + t)
    return lane, i1, i2, w1, w2


R_E1, R_E2, R_W1, R_W2, R_RANK1, R_RANK2 = range(6)


def _row_copy(src, row, dst, dst_row, sem):
    return pltpu.make_async_copy(src.at[pl.ds(row, 1)], dst.at[pl.ds(dst_row, 1)], sem)


def _post_kernel(x_ref, ya_ref, yb_ref, ga_ref, gb_ref, pa_ref, pb_ref, wo_ref, gain_ref,
                 wrh_ref, wrl_ref, br_ref, tri_ref, xmid_ref, rinfo_ref, counts_ref, dest_ref,
                 xs_hbm, h_scr, idx_vmem, idx_smem, row_sem, idx_sem, *, capacity, dump_row):
    i = pl.program_id(0)
    n_steps = pl.num_programs(0)
    slot = i % 2
    rows = x_ref.shape[0]

    def dispatch(s):
        for r in range(rows):
            for c in range(2):
                _row_copy(h_scr.at[s], r, xs_hbm, idx_smem[s, c, r], row_sem.at[c]).start(priority=c)

    def wait_dispatch(s):
        for c in range(2):
            pltpu.make_async_copy(h_scr.at[s], xs_hbm.at[pl.ds(0, rows)], row_sem.at[c]).wait()

    @pl.when(i == 0)
    def _():
        counts_ref[...] = jnp.zeros_like(counts_ref)
        h_scr[1] = jnp.zeros(h_scr.shape[1:], F32)

        def fill(r, carry):
            idx_smem[1, 0, r] = dump_row + r
            idx_smem[1, 1, r] = dump_row + rows + r
            return carry
        lax.fori_loop(0, rows, fill, 0)

    dispatch(1 - slot)

    ga = jax.nn.sigmoid(ga_ref[...].astype(F32))
    gb = jax.nn.sigmoid(gb_ref[...].astype(F32))
    merged = ga * _dot(ya_ref[...], pa_ref[...]) + gb * _dot(yb_ref[...], pb_ref[...])
    x_mid = x_ref[...] + _dot(merged.astype(BF16), wo_ref[...])
    xmid_ref[...] = x_mid
    h = _rms(x_mid, gain_ref[...])
    h_scr[slot] = h
    logits = _dot_split(h, wrh_ref[...], wrl_ref[...]) + br_ref[...]
    lane, i1, i2, w1, w2 = _route(logits)

    onehot = jnp.where((lane == i1) | (lane == i2), 1.0, 0.0)
    total = _dot(tri_ref[...], onehot.astype(BF16)) + counts_ref[...]
    rank1 = jnp.sum(jnp.where(lane == i1, total, 0.0), axis=-1, keepdims=True)
    rank2 = jnp.sum(jnp.where(lane == i2, total, 0.0), axis=-1, keepdims=True)
    counts_ref[...] += jnp.sum(onehot, axis=0, keepdims=True)

    rec = jnp.zeros(logits.shape, F32)
    for lane_id, val in ((R_E1, i1.astype(F32)), (R_E2, i2.astype(F32)), (R_W1, w1), (R_W2, w2),
                         (R_RANK1, rank1), (R_RANK2, rank2)):
        rec = jnp.where(lane == lane_id, val, rec)
    rinfo_ref[...] = rec

    dest1 = (i1 - N_GROUPS) * capacity + rank1.astype(jnp.int32)
    dest2 = (i2 - N_GROUPS) * capacity + rank2.astype(jnp.int32)
    dest = jnp.where(lane == 0, dest1, jnp.where(lane == 1, dest2, 0))
    dest_rows = jnp.transpose(dest, (1, 0))[0:idx_vmem.shape[0], :]
    idx_vmem[...] = dest_rows
    dest_ref[0] = dest_rows
    idx_copy = pltpu.make_async_copy(idx_vmem, idx_smem.at[slot], idx_sem)
    idx_copy.start()
    idx_copy.wait()

    wait_dispatch(1 - slot)

    @pl.when(i == n_steps - 1)
    def _():
        dispatch(slot)
        wait_dispatch(slot)


def _post(x2d, ya, yb, u2d, pa, pb, wo, gain, wr, br, *, tm, n_experts, ga_col, gb_col):
    t, d = x2d.shape
    aw, bw = ya.shape[1], yb.shape[1]
    wr_hi, wr_lo = _split_bf16(wr)
    tri = jnp.tril(jnp.ones((tm, tm), BF16), k=-1)
    capacity = t
    dump_row = n_experts * capacity
    sublanes = 8
    kern = functools.partial(_post_kernel, capacity=capacity, dump_row=dump_row)
    return pl.pallas_call(
        kern,
        grid=(t // tm,),
        in_specs=[
            pl.BlockSpec((tm, d), lambda i: (i, 0)),
            pl.BlockSpec((tm, aw), lambda i: (i, 0)),
            pl.BlockSpec((tm, bw), lambda i: (i, 0)),
            pl.BlockSpec((tm, d), lambda i: (i, ga_col)),
            pl.BlockSpec((tm, d), lambda i: (i, gb_col)),
            _const_spec(pa.shape), _const_spec(pb.shape), _const_spec(wo.shape),
            _const_spec(gain.shape), _const_spec(wr.shape), _const_spec(wr.shape),
            _const_spec(br.shape), _const_spec(tri.shape),
        ],
        out_specs=[
            pl.BlockSpec((tm, d), lambda i: (i, 0)),
            pl.BlockSpec((tm, LANES), lambda i: (i, 0)),
            pl.BlockSpec((1, LANES), lambda i: (0, 0)),
            pl.BlockSpec((1, sublanes, tm), lambda i: (i, 0, 0)),
            pl.BlockSpec(memory_space=pl.ANY),
        ],
        out_shape=[
            jax.ShapeDtypeStruct((t, d), F32),
            jax.ShapeDtypeStruct((t, LANES), F32),
            jax.ShapeDtypeStruct((1, LANES), F32),
            jax.ShapeDtypeStruct((t // tm, sublanes, tm), jnp.int32),
            jax.ShapeDtypeStruct((dump_row + 2 * tm, d), F32),
        ],
        scratch_shapes=[
            pltpu.VMEM((2, tm, d), F32),
            pltpu.VMEM((sublanes, tm), jnp.int32),
            pltpu.SMEM((2, sublanes, tm), jnp.int32),
            pltpu.SemaphoreType.DMA((2,)),
            pltpu.SemaphoreType.DMA(()),
        ],
        compiler_params=_params(("arbitrary",)),
        name="merge_route",
    )(x2d, ya, yb, u2d, u2d, pa, pb, wo, gain, wr_hi, wr_lo, br, tri)


MOE_TILE = 256


def _start_row_copy(copy, index):
    copy.start(priority=index % 2)


def _moe_kernel(tile_block_ref, tile_expert_ref, tile_valid_ref, x_ref, wg_ref, wu_ref, wd_ref,
                ys_ref):
    del tile_block_ref, tile_expert_ref
    valid = tile_valid_ref[pl.program_id(0)]
    row = lax.broadcasted_iota(jnp.int32, (x_ref.shape[0], 1), 0)
    x = jnp.where(row < valid, x_ref[...], 0.0).astype(BF16)
    a = _dot(x, wg_ref[...].astype(BF16))
    a = a * jax.nn.sigmoid(a) * _dot(x, wu_ref[...].astype(BF16))
    ys_ref[...] = _dot(a.astype(BF16), wd_ref[...].astype(BF16))


def _moe_experts(tile_block, tile_expert, tile_valid, xs, wg, wu, wd):
    n_tiles = tile_block.shape[0]
    _, d = xs.shape
    _, _, de = wg.shape
    grid_spec = pltpu.PrefetchScalarGridSpec(
        num_scalar_prefetch=3,
        grid=(n_tiles,),
        in_specs=[
            pl.BlockSpec((MOE_TILE, d), lambda i, tb, te, tv: (tb[i], 0)),
            pl.BlockSpec((None, d, de), lambda i, tb, te, tv: (te[i], 0, 0)),
            pl.BlockSpec((None, d, de), lambda i, tb, te, tv: (te[i], 0, 0)),
            pl.BlockSpec((None, de, d), lambda i, tb, te, tv: (te[i], 0, 0)),
        ],
        out_specs=pl.BlockSpec((MOE_TILE, d), lambda i, tb, te, tv: (i, 0)),
    )
    return pl.pallas_call(
        _moe_kernel,
        grid_spec=grid_spec,
        out_shape=jax.ShapeDtypeStruct((n_tiles * MOE_TILE, d), F32),
        compiler_params=_params(("arbitrary",)),
        name="moe_experts",
    )(tile_block, tile_expert, tile_valid, xs, wg, wu, wd)


def _combine_kernel(pos_ref, xmid_ref, rinfo_ref, ys_hbm, o_ref, b1_ref, b2_ref, sem_ref):
    rows = o_ref.shape[0]

    def copies(r):
        return (_row_copy(ys_hbm, pos_ref[0, 0, r], b1_ref, r, sem_ref.at[0]),
                _row_copy(ys_hbm, pos_ref[0, 0, rows + r], b2_ref, r, sem_ref.at[1]))

    for r in range(rows):
        c1, c2 = copies(r)
        _start_row_copy(c1, 0)
        _start_row_copy(c2, 1)
    pltpu.make_async_copy(ys_hbm.at[pl.ds(0, rows)], b1_ref, sem_ref.at[0]).wait()
    pltpu.make_async_copy(ys_hbm.at[pl.ds(0, rows)], b2_ref, sem_ref.at[1]).wait()

    rinfo = rinfo_ref[...]
    w1 = rinfo[:, R_W1:R_W1 + 1]
    w2 = rinfo[:, R_W2:R_W2 + 1]
    o_ref[...] = xmid_ref[...] + w1 * b1_ref[...] + w2 * b2_ref[...]


def _combine(pos3, xmid, rinfo, ys, *, tc):
    t, d = xmid.shape
    return pl.pallas_call(
        _combine_kernel,
        grid=(t // tc,),
        in_specs=[
            pl.BlockSpec((1, 1, 2 * tc), lambda i: (i, 0, 0), memory_space=pltpu.SMEM),
            pl.BlockSpec((tc, d), lambda i: (i, 0)),
            pl.BlockSpec((tc, LANES), lambda i: (i, 0)),
            pl.BlockSpec(memory_space=pl.ANY),
        ],
        out_specs=pl.BlockSpec((tc, d), lambda i: (i, 0)),
        out_shape=jax.ShapeDtypeStruct((t, d), F32),
        scratch_shapes=[pltpu.VMEM((tc, d), F32), pltpu.VMEM((tc, d), F32),
                        pltpu.SemaphoreType.DMA((2,))],
        compiler_params=_params(("arbitrary",)),
        name="moe_combine",
    )(pos3, xmid, rinfo, ys)


def _moe(x_mid, xs, rinfo, counts_row, dest, wg, wu, wd):
    t = x_mid.shape[0]
    ne = wg.shape[0]
    capacity = t
    assert capacity % MOE_TILE == 0
    n_tiles = (2 * t) // MOE_TILE + ne
    dest = dest[:, 0:2, :]
    e_id = dest // capacity
    rank = dest - e_id * capacity
    counts = counts_row[0, N_GROUPS:N_GROUPS + ne].astype(jnp.int32)
    tiles_per = (counts + MOE_TILE - 1) // MOE_TILE
    tile_end = jnp.cumsum(tiles_per)
    tile_first = tile_end - tiles_per
    tile = jnp.arange(n_tiles, dtype=jnp.int32)
    tile_expert = jnp.minimum(
        jnp.sum((tile_end[None, :] <= tile[:, None]).astype(jnp.int32), axis=1), ne - 1)
    local = tile - tile_first[tile_expert]
    used = tile < tile_end[-1]
    tile_block = jnp.where(used, tile_expert * (capacity // MOE_TILE) + local, 0)
    tile_valid = jnp.where(used, jnp.clip(counts[tile_expert] - local * MOE_TILE, 0, MOE_TILE), 0)
    ys = _moe_experts(tile_block, tile_expert, tile_valid, xs, wg, wu, wd)
    pos = (tile_first * MOE_TILE)[e_id] + rank
    steps, _, tc = dest.shape
    return _combine(pos.reshape(steps, 1, 2 * tc), x_mid, rinfo, ys, tc=tc)


def kernel(x, norm_mix_gain, w_in, a_q_gain, a_k_gain, rel_bias, b_gate_up, b_gate_bias,
           b_out_gain, proj_a, proj_b, w_out, norm_ffn_gain, w_group, b_group, w_router,
           b_router, w_gate, w_up, w_down):
    bsz, seq, d = x.shape
    depth = w_in.shape[0]
    t = bsz * seq
    a_width = proj_a.shape[1]
    a_hd = a_width // A_HEADS
    b_vw = proj_b.shape[1]
    b_qkw = b_gate_up.shape[2]
    dk, dv = b_qkw // B_HEADS, b_vw // B_HEADS
    assert seq % ATT_Q == 0 and seq % GLA_ROWS == 0 and d == 2 * a_width == 2 * b_vw
    assert b_qkw * 2 == a_width and N_GROUPS + N_EXPERTS <= LANES

    o_lr = 3 * a_width + 2 * b_qkw + 2 * b_vw
    qk_col = 3 * a_width // b_qkw
    v_col = (3 * a_width + 2 * b_qkw) // b_vw
    r_col = v_col + 1
    ga_col = (3 * a_width + 2 * b_qkw + 2 * b_vw) // d
    gb_col = ga_col + 1
    assert (3 * a_width + 2 * b_qkw + 2 * b_vw) % d == 0

    head_id = jnp.arange(a_width) // a_hd
    ones_bd = (head_id[:, None] == head_id[None, :]).astype(BF16)
    tri = jnp.tril(jnp.ones((CHUNK, CHUNK), BF16))
    bias_tile = _attn_bias_tile(rel_bias)

    x2d = x.reshape(t, d)
    tm = 512
    for l in range(depth):
        w_main = jnp.concatenate([w_in[l][:, :o_lr], w_in[l][:, o_lr + B_GATE_RANK:]],
                                 axis=1).astype(BF16)
        w_lr = jnp.pad(w_in[l][:, o_lr:o_lr + B_GATE_RANK],
                       ((0, 0), (0, LANES - B_GATE_RANK))).astype(BF16)
        u, lr = _inproj(x2d, norm_mix_gain[l][None, :], w_main, w_lr,
                        a_q_gain[l].reshape(1, a_width), a_k_gain[l].reshape(1, a_width),
                        ones_bd, tm=tm, a_width=a_width, head_dim=a_hd)
        u3 = u.reshape(bsz, seq, u.shape[1])
        y_a = _attention(u3, bias_tile, heads=A_HEADS, head_dim=a_hd)
        gup_pad = jnp.pad(b_gate_up[l], ((0, LANES - B_GATE_RANK), (0, 0)))
        y_b = _gla(u3, lr.reshape(bsz, seq, LANES), gup_pad, b_gate_bias[l][None, :],
                   b_out_gain[l].reshape(1, b_vw), tri, heads=B_HEADS, dk=dk, dv=dv,
                   qk_col=qk_col, v_col=v_col, r_col=r_col)
        w_r = jnp.pad(jnp.concatenate([w_group[l], w_router[l]], axis=1),
                      ((0, 0), (0, LANES - N_GROUPS - N_EXPERTS)))
        b_r = jnp.pad(jnp.concatenate([b_group[l], b_router[l]]),
                      (0, LANES - N_GROUPS - N_EXPERTS))[None, :]
        x_mid, rinfo, counts, dest, xs = _post(
            x2d, y_a.reshape(t, a_width), y_b.reshape(t, b_vw), u, proj_a[l].astype(BF16),
            proj_b[l].astype(BF16), w_out[l].astype(BF16), norm_ffn_gain[l][None, :], w_r, b_r,
            tm=tm, n_experts=N_EXPERTS, ga_col=ga_col, gb_col=gb_col)
        x2d = _moe(x_mid, xs, rinfo, counts, dest, w_gate[l], w_up[l], w_down[l])
    return x2d.reshape(bsz, seq, d)
```

```python
import functools

import jax
import jax.numpy as jnp
from jax import lax
from jax.experimental import pallas as pl
from jax.experimental.pallas import tpu as pltpu

F32 = jnp.float32
BF16 = jnp.bfloat16

EPS = 1e-6
CHUNK = 64
A_HEADS = 8
A_LEFT_CHUNKS = 8
REL_CLIP = 2 * CHUNK
B_HEADS = 4
B_GATE_RANK = 16
B_GATE_TEMP = 16.0
N_GROUPS = 4
EXPERTS_PER_GROUP = 8
N_EXPERTS = N_GROUPS * EXPERTS_PER_GROUP

LANES = 128
LOG2E = 1.4426950408889634
MASK_NEG = -1e30
VMEM_LIMIT = 56 * 1024 * 1024

ATT_Q = 256
ATT_K = ATT_Q + A_LEFT_CHUNKS * CHUNK
GLA_ROWS = 256
SUB = 16


def _params(sem):
    return pltpu.CompilerParams(dimension_semantics=sem, vmem_limit_bytes=VMEM_LIMIT)


def _const_spec(shape):
    nd = len(shape)
    return pl.BlockSpec(shape, lambda *_: (0,) * nd, pipeline_mode=pl.Buffered(1))


def _dot(a, b):
    return jnp.dot(a, b, preferred_element_type=F32)


def _dot_nt(a, b):
    return lax.dot_general(a, b, (((1,), (1,)), ((), ())), preferred_element_type=F32)


def _dot_tn(a, b):
    return lax.dot_general(a, b, (((0,), (0,)), ((), ())), preferred_element_type=F32)


def _split_bf16(w):
    hi = w.astype(BF16)
    return hi, (w - hi.astype(F32)).astype(BF16)


def _dot_split(x, w_hi, w_lo):
    x_hi, x_lo = _split_bf16(x)
    return _dot(x_hi, w_hi) + _dot(x_lo, w_hi) + _dot(x_hi, w_lo)


def _rms(x, gain):
    return x * lax.rsqrt(jnp.mean(x * x, axis=-1, keepdims=True) + EPS) * gain


def _head_norm(acc, ones_bd, gain, head_dim, scale):
    ss = _dot((acc * acc).astype(BF16), ones_bd)
    inv = lax.rsqrt(ss * (1.0 / head_dim) + EPS)
    return acc * inv * (gain * scale)


def _inproj_kernel(x_ref, gain_ref, w_ref, wlr_ref, qg_ref, kg_ref, ones_ref,
                   u_ref, lr_ref, *, a_width, head_dim, col_chunk):
    x = x_ref[...]
    h = _rms(x, gain_ref[...]).astype(BF16)
    n_cols = w_ref.shape[1]
    for c0 in range(0, n_cols, col_chunk):
        acc = _dot(h, w_ref[:, c0:c0 + col_chunk])
        if c0 == 0:
            acc = _head_norm(acc, ones_ref[...], qg_ref[...], head_dim, head_dim ** -0.5 * LOG2E)
        elif c0 == a_width:
            acc = _head_norm(acc, ones_ref[...], kg_ref[...], head_dim, 1.0)
        u_ref[:, c0:c0 + col_chunk] = acc.astype(BF16)
    lr_ref[...] = _dot(h, wlr_ref[...])


def _inproj(x2d, gain, w_main, w_lr, q_gain, k_gain, ones_bd, *, tm, a_width, head_dim):
    t, d = x2d.shape
    n = w_main.shape[1]
    kern = functools.partial(_inproj_kernel, a_width=a_width, head_dim=head_dim,
                             col_chunk=a_width)
    return pl.pallas_call(
        kern,
        grid=(t // tm,),
        in_specs=[
            pl.BlockSpec((tm, d), lambda i: (i, 0)),
            _const_spec((1, d)),
            _const_spec((d, n)),
            _const_spec((d, LANES)),
            _const_spec((1, a_width)),
            _const_spec((1, a_width)),
            _const_spec((a_width, a_width)),
        ],
        out_specs=[
            pl.BlockSpec((tm, n), lambda i: (i, 0)),
            pl.BlockSpec((tm, LANES), lambda i: (i, 0)),
        ],
        out_shape=[
            jax.ShapeDtypeStruct((t, n), BF16),
            jax.ShapeDtypeStruct((t, LANES), F32),
        ],
        compiler_params=_params(("parallel",)),
        name="inproj",
    )(x2d, gain, w_main, w_lr, q_gain, k_gain, ones_bd)


def _attn_kernel(q_ref, k0_ref, k1_ref, k2_ref, v0_ref, v1_ref, v2_ref, bias_ref, o_ref,
                 *, heads, head_dim):
    i = pl.program_id(1)
    k_refs = (k0_ref, k1_ref, k2_ref)
    v_refs = (v0_ref, v1_ref, v2_ref)
    nq = q_ref.shape[0]
    ones = jnp.ones((nq, head_dim), BF16)

    def body(first_valid):
        for h in range(heads):
            hs = slice(h * head_dim, (h + 1) * head_dim)
            q = q_ref[:, hs]
            s = [_dot_nt(q, k_refs[j][:, hs]) + bias_ref[h, :, j * nq:(j + 1) * nq]
                 for j in range(first_valid, 3)]
            m = jnp.max(s[0], axis=-1, keepdims=True)
            for sj in s[1:]:
                m = jnp.maximum(m, jnp.max(sj, axis=-1, keepdims=True))
            o = None
            for j, sj in zip(range(first_valid, 3), s):
                p = jnp.exp2(sj - m).astype(BF16)
                v_aug = jnp.concatenate([v_refs[j][:, hs], ones], axis=1)
                oj = _dot(p, v_aug)
                o = oj if o is None else o + oj
            o_ref[:, hs] = (o[:, :head_dim] / o[:, head_dim:head_dim + 1]).astype(BF16)

    for fv in range(3):
        @pl.when(jnp.maximum(2 - i, 0) == fv)
        def _(fv=fv):
            body(fv)


def _attention(u3, bias_tile, *, heads, head_dim):
    b, s, _ = u3.shape
    width = heads * head_dim
    nblk = s // ATT_Q
    kern = functools.partial(_attn_kernel, heads=heads, head_dim=head_dim)

    def kv_spec(col, back):
        return pl.BlockSpec((None, ATT_Q, width),
                            lambda bi, i: (bi, jnp.maximum(i - back, 0), col))

    return pl.pallas_call(
        kern,
        grid=(b, nblk),
        in_specs=[
            pl.BlockSpec((None, ATT_Q, width), lambda bi, i: (bi, i, 0)),
            kv_spec(1, 2), kv_spec(1, 1), kv_spec(1, 0),
            kv_spec(2, 2), kv_spec(2, 1), kv_spec(2, 0),
            _const_spec(bias_tile.shape),
        ],
        out_specs=pl.BlockSpec((None, ATT_Q, width), lambda bi, i: (bi, i, 0)),
        out_shape=jax.ShapeDtypeStruct((b, s, width), BF16),
        compiler_params=_params(("parallel", "parallel")),
        name="band_attn",
    )(u3, u3, u3, u3, u3, u3, u3, bias_tile)


def _attn_bias_tile(rel_bias):
    pad = A_LEFT_CHUNKS * CHUNK
    r = jnp.arange(ATT_Q)[:, None]
    c = jnp.arange(ATT_K)[None, :]
    heads = rel_bias.shape[1]
    rb = rel_bias.astype(F32).T
    hi, lo = rb[:, -1:], rb[:, :1]
    g = jnp.concatenate([
        jnp.broadcast_to(hi, (heads, pad - REL_CLIP + 1)),
        rb[:, 2 * REL_CLIP - 1:0:-1],
        jnp.broadcast_to(lo, (heads, ATT_K - pad - REL_CLIP)),
        lo,
        jnp.broadcast_to(hi, (heads, ATT_Q - 1)),
    ], axis=1)
    length = ATT_Q + ATT_K
    skew = jnp.broadcast_to(g[:, None, :], (heads, ATT_Q, length)).reshape(heads, -1)
    skew = skew[:, :ATT_Q * (length - 1)].reshape(heads, ATT_Q, length - 1)
    bias = skew[:, :, :ATT_K]
    qc, kc = r // CHUNK, c // CHUNK
    allowed = (kc >= qc) & (kc <= qc + A_LEFT_CHUNKS)
    return jnp.where(allowed[None], bias * LOG2E, MASK_NEG)


def _log_sigmoid(z):
    return jnp.minimum(z, 0.0) - jnp.log(1.0 + jnp.exp(-jnp.abs(z)))


def _gla_kernel(q_ref, k_ref, v_ref, r_ref, lr_ref, guph_ref, gupl_ref, gbias_ref, ogain_ref, tri_ref,
                y_ref, state_ref, *, heads, dk, dv):
    @pl.when(pl.program_id(1) == 0)
    def _():
        state_ref[...] = jnp.zeros_like(state_ref)

    rows = q_ref.shape[0]
    qk_w = heads * dk
    row_id = lax.broadcasted_iota(jnp.int32, (CHUNK, qk_w), 0)
    ci = lax.broadcasted_iota(jnp.int32, (CHUNK, CHUNK), 0)
    cj = lax.broadcasted_iota(jnp.int32, (CHUNK, CHUNK), 1)
    causal = ci >= cj
    n_sub = CHUNK // SUB

    for c in range(rows // CHUNK):
        rs = slice(c * CHUNK, (c + 1) * CHUNK)
        z = _dot_split(lr_ref[rs, :], guph_ref[...], gupl_ref[...]) + gbias_ref[...]
        log_a = _log_sigmoid(z) * (1.0 / B_GATE_TEMP)
        la_hi = log_a.astype(BF16)
        la_r1 = log_a - la_hi.astype(F32)
        la_mid = la_r1.astype(BF16)
        la_lo = (la_r1 - la_mid.astype(F32)).astype(BF16)
        b = _dot(tri_ref[...], la_hi) + _dot(tri_ref[...], la_mid) + _dot(tri_ref[...], la_lo)
        q = q_ref[rs, :].astype(F32) * (dk ** -0.5)
        k = k_ref[rs, :].astype(F32)
        b_last = b[CHUNK - 1:CHUNK, :]

        ref_rows = [b[(j + 1) * SUB - 1:(j + 1) * SUB, :] for j in range(n_sub)]
        ref_full = jnp.concatenate(
            [jnp.broadcast_to(rj, (SUB, qk_w)) for rj in ref_rows], axis=0)
        k_dec = k * jnp.exp(ref_full - b)
        q_dec = [q * jnp.exp(jnp.where(row_id >= j * SUB, b - ref_rows[j], -jnp.inf))
                 for j in range(n_sub)]
        k_sub = [jnp.where((row_id >= j * SUB) & (row_id < (j + 1) * SUB), k_dec, 0.0)
                 for j in range(n_sub)]
        q_in = (q * jnp.exp(b)).astype(BF16)
        k_out = (k * jnp.exp(b_last - b)).astype(BF16)
        e_last = jnp.exp(b_last)

        for h in range(heads):
            ks = slice(h * dk, (h + 1) * dk)
            vs = slice(h * dv, (h + 1) * dv)
            q_cat = jnp.concatenate([qd[:, ks] for qd in q_dec], axis=1).astype(BF16)
            k_cat = jnp.concatenate([kz[:, ks] for kz in k_sub], axis=1).astype(BF16)
            attn = jnp.where(causal, _dot_nt(q_cat, k_cat), 0.0)
            v_h = v_ref[rs, vs]
            st = state_ref[h]
            o = _dot(attn.astype(BF16), v_h) + _dot_nt(q_in[:, ks], st.astype(BF16))
            state_ref[h] = st * e_last[:, ks] + _dot_tn(v_h, k_out[:, ks])
            r_h = r_ref[rs, vs].astype(F32)
            y = _rms(o, ogain_ref[:, vs]) * (r_h * jax.nn.sigmoid(r_h))
            y_ref[rs, vs] = y.astype(BF16)


def _gla(u3, lr3, gup_pad, gbias, ogain, tri, *, heads, dk, dv, qk_col, v_col, r_col):
    b, s, _ = u3.shape
    qk_w, v_w = heads * dk, heads * dv
    kern = functools.partial(_gla_kernel, heads=heads, dk=dk, dv=dv)
    return pl.pallas_call(
        kern,
        grid=(b, s // GLA_ROWS),
        in_specs=[
            pl.BlockSpec((None, GLA_ROWS, qk_w), lambda bi, i: (bi, i, qk_col)),
            pl.BlockSpec((None, GLA_ROWS, qk_w), lambda bi, i: (bi, i, qk_col + 1)),
            pl.BlockSpec((None, GLA_ROWS, v_w), lambda bi, i: (bi, i, v_col)),
            pl.BlockSpec((None, GLA_ROWS, v_w), lambda bi, i: (bi, i, r_col)),
            pl.BlockSpec((None, GLA_ROWS, LANES), lambda bi, i: (bi, i, 0)),
            _const_spec(gup_pad.shape),
            _const_spec(gup_pad.shape),
            _const_spec(gbias.shape),
            _const_spec(ogain.shape),
            _const_spec(tri.shape),
        ],
        out_specs=pl.BlockSpec((None, GLA_ROWS, v_w), lambda bi, i: (bi, i, 0)),
        out_shape=jax.ShapeDtypeStruct((b, s, v_w), BF16),
        scratch_shapes=[pltpu.VMEM((heads, dv, dk), F32)],
        compiler_params=_params(("parallel", "arbitrary")),
        name="gla",
    )(u3, u3, u3, u3, lr3, *_split_bf16(gup_pad), gbias, ogain, tri)


def _route(logits):
    lane = lax.broadcasted_iota(jnp.int32, logits.shape, 1)
    neg_inf = -jnp.inf
    gl = jnp.where(lane < N_GROUPS, logits, neg_inf)
    gmax = jnp.max(gl, axis=-1, keepdims=True)
    g_idx = jnp.min(jnp.where(gl == gmax, lane, LANES), axis=-1, keepdims=True)
    g_top = 1.0 / jnp.sum(jnp.exp(gl - gmax), axis=-1, keepdims=True)
    e_lane = lane - N_GROUPS
    in_group = (e_lane >= g_idx * EXPERTS_PER_GROUP) & (e_lane < (g_idx + 1) * EXPERTS_PER_GROUP)
    el = jnp.where(in_group, logits, neg_inf)
    e1 = jnp.max(el, axis=-1, keepdims=True)
    i1 = jnp.min(jnp.where(el == e1, lane, LANES), axis=-1, keepdims=True)
    el2 = jnp.where(lane == i1, neg_inf, el)
    e2 = jnp.max(el2, axis=-1, keepdims=True)
    i2 = jnp.min(jnp.where(el2 == e2, lane, LANES), axis=-1, keepdims=True)
    t = jnp.exp(e2 - e1)
    w1 = g_top / (1.0 + t)
    w2 = g_top * t / (1.0 + t)
    return lane, i1, i2, w1, w2


R_E1, R_E2, R_W1, R_W2, R_RANK1, R_RANK2 = range(6)


def _row_copy(src, row, dst, dst_row, sem):
    return pltpu.make_async_copy(src.at[pl.ds(row, 1)], dst.at[pl.ds(dst_row, 1)], sem)


def _post_kernel(x_ref, ya_ref, yb_ref, ga_ref, gb_ref, pa_ref, pb_ref, wo_ref, gain_ref,
                 wrh_ref, wrl_ref, br_ref, tri_ref, xmid_ref, rinfo_ref, counts_ref, dest_ref,
                 xs_hbm, h_scr, idx_vmem, idx_smem, row_sem, idx_sem, *, capacity, dump_row):
    i = pl.program_id(0)
    n_steps = pl.num_programs(0)
    slot = i % 2
    rows = x_ref.shape[0]

    def dispatch(s):
        for r in range(rows):
            for c in range(2):
                _row_copy(h_scr.at[s], r, xs_hbm, idx_smem[s, c, r], row_sem.at[c]).start(priority=c)

    def wait_dispatch(s):
        for c in range(2):
            pltpu.make_async_copy(h_scr.at[s], xs_hbm.at[pl.ds(0, rows)], row_sem.at[c]).wait()

    @pl.when(i == 0)
    def _():
        counts_ref[...] = jnp.zeros_like(counts_ref)
        h_scr[1] = jnp.zeros(h_scr.shape[1:], F32)

        def fill(r, carry):
            idx_smem[1, 0, r] = dump_row + r
            idx_smem[1, 1, r] = dump_row + rows + r
            return carry
        lax.fori_loop(0, rows, fill, 0)

    dispatch(1 - slot)

    ga = jax.nn.sigmoid(ga_ref[...].astype(F32))
    gb = jax.nn.sigmoid(gb_ref[...].astype(F32))
    merged = ga * _dot(ya_ref[...], pa_ref[...]) + gb * _dot(yb_ref[...], pb_ref[...])
    x_mid = x_ref[...] + _dot(merged.astype(BF16), wo_ref[...])
    xmid_ref[...] = x_mid
    h = _rms(x_mid, gain_ref[...])
    h_scr[slot] = h
    logits = _dot_split(h, wrh_ref[...], wrl_ref[...]) + br_ref[...]
    lane, i1, i2, w1, w2 = _route(logits)

    onehot = jnp.where((lane == i1) | (lane == i2), 1.0, 0.0)
    total = _dot(tri_ref[...], onehot.astype(BF16)) + counts_ref[...]
    rank1 = jnp.sum(jnp.where(lane == i1, total, 0.0), axis=-1, keepdims=True)
    rank2 = jnp.sum(jnp.where(lane == i2, total, 0.0), axis=-1, keepdims=True)
    counts_ref[...] += jnp.sum(onehot, axis=0, keepdims=True)

    rec = jnp.zeros(logits.shape, F32)
    for lane_id, val in ((R_E1, i1.astype(F32)), (R_E2, i2.astype(F32)), (R_W1, w1), (R_W2, w2),
                         (R_RANK1, rank1), (R_RANK2, rank2)):
        rec = jnp.where(lane == lane_id, val, rec)
    rinfo_ref[...] = rec

    dest1 = (i1 - N_GROUPS) * capacity + rank1.astype(jnp.int32)
    dest2 = (i2 - N_GROUPS) * capacity + rank2.astype(jnp.int32)
    dest = jnp.where(lane == 0, dest1, jnp.where(lane == 1, dest2, 0))
    dest_rows = jnp.transpose(dest, (1, 0))[0:idx_vmem.shape[0], :]
    idx_vmem[...] = dest_rows
    dest_ref[0] = dest_rows
    idx_copy = pltpu.make_async_copy(idx_vmem, idx_smem.at[slot], idx_sem)
    idx_copy.start()
    idx_copy.wait()

    wait_dispatch(1 - slot)

    @pl.when(i == n_steps - 1)
    def _():
        dispatch(slot)
        wait_dispatch(slot)


def _post(x2d, ya, yb, u2d, pa, pb, wo, gain, wr, br, *, tm, n_experts, ga_col, gb_col):
    t, d = x2d.shape
    aw, bw = ya.shape[1], yb.shape[1]
    wr_hi, wr_lo = _split_bf16(wr)
    tri = jnp.tril(jnp.ones((tm, tm), BF16), k=-1)
    capacity = t
    dump_row = n_experts * capacity
    sublanes = 8
    kern = functools.partial(_post_kernel, capacity=capacity, dump_row=dump_row)
    return pl.pallas_call(
        kern,
        grid=(t // tm,),
        in_specs=[
            pl.BlockSpec((tm, d), lambda i: (i, 0)),
            pl.BlockSpec((tm, aw), lambda i: (i, 0)),
            pl.BlockSpec((tm, bw), lambda i: (i, 0)),
            pl.BlockSpec((tm, d), lambda i: (i, ga_col)),
            pl.BlockSpec((tm, d), lambda i: (i, gb_col)),
            _const_spec(pa.shape), _const_spec(pb.shape), _const_spec(wo.shape),
            _const_spec(gain.shape), _const_spec(wr.shape), _const_spec(wr.shape),
            _const_spec(br.shape), _const_spec(tri.shape),
        ],
        out_specs=[
            pl.BlockSpec((tm, d), lambda i: (i, 0)),
            pl.BlockSpec((tm, LANES), lambda i: (i, 0)),
            pl.BlockSpec((1, LANES), lambda i: (0, 0)),
            pl.BlockSpec((1, sublanes, tm), lambda i: (i, 0, 0)),
            pl.BlockSpec(memory_space=pl.ANY),
        ],
        out_shape=[
            jax.ShapeDtypeStruct((t, d), F32),
            jax.ShapeDtypeStruct((t, LANES), F32),
            jax.ShapeDtypeStruct((1, LANES), F32),
            jax.ShapeDtypeStruct((t // tm, sublanes, tm), jnp.int32),
            jax.ShapeDtypeStruct((dump_row + 2 * tm, d), F32),
        ],
        scratch_shapes=[
            pltpu.VMEM((2, tm, d), F32),
            pltpu.VMEM((sublanes, tm), jnp.int32),
            pltpu.SMEM((2, sublanes, tm), jnp.int32),
            pltpu.SemaphoreType.DMA((2,)),
            pltpu.SemaphoreType.DMA(()),
        ],
        compiler_params=_params(("arbitrary",)),
        name="merge_route",
    )(x2d, ya, yb, u2d, u2d, pa, pb, wo, gain, wr_hi, wr_lo, br, tri)


MOE_TILE = 256


def _start_row_copy(copy, index):
    copy.start(priority=index % 2)


def _moe_kernel(tile_block_ref, tile_expert_ref, tile_valid_ref, x_ref, wg_ref, wu_ref, wd_ref,
                ys_ref):
    del tile_block_ref, tile_expert_ref
    valid = tile_valid_ref[pl.program_id(0)]
    row = lax.broadcasted_iota(jnp.int32, (x_ref.shape[0], 1), 0)
    x = jnp.where(row < valid, x_ref[...], 0.0).astype(BF16)
    a = _dot(x, wg_ref[...].astype(BF16))
    a = a * jax.nn.sigmoid(a) * _dot(x, wu_ref[...].astype(BF16))
    ys_ref[...] = _dot(a.astype(BF16), wd_ref[...].astype(BF16))


def _moe_experts(tile_block, tile_expert, tile_valid, xs, wg, wu, wd, layer):
    n_tiles = tile_block.shape[0]
    _, d = xs.shape
    _, _, _, de = wg.shape
    grid_spec = pltpu.PrefetchScalarGridSpec(
        num_scalar_prefetch=3,
        grid=(n_tiles,),
        in_specs=[
            pl.BlockSpec((MOE_TILE, d), lambda i, tb, te, tv: (tb[i], 0)),
            pl.BlockSpec((None, None, d, de), lambda i, tb, te, tv: (layer, te[i], 0, 0)),
            pl.BlockSpec((None, None, d, de), lambda i, tb, te, tv: (layer, te[i], 0, 0)),
            pl.BlockSpec((None, None, de, d), lambda i, tb, te, tv: (layer, te[i], 0, 0)),
        ],
        out_specs=pl.BlockSpec((MOE_TILE, d), lambda i, tb, te, tv: (i, 0)),
    )
    return pl.pallas_call(
        _moe_kernel,
        grid_spec=grid_spec,
        out_shape=jax.ShapeDtypeStruct((n_tiles * MOE_TILE, d), F32),
        compiler_params=_params(("arbitrary",)),
        name="moe_experts",
    )(tile_block, tile_expert, tile_valid, xs, wg, wu, wd)


def _combine_kernel(pos_ref, xmid_ref, rinfo_ref, ys_hbm, o_ref, b1_ref, b2_ref, sem_ref):
    rows = o_ref.shape[0]

    def copies(r):
        return (_row_copy(ys_hbm, pos_ref[0, 0, r], b1_ref, r, sem_ref.at[0]),
                _row_copy(ys_hbm, pos_ref[0, 0, rows + r], b2_ref, r, sem_ref.at[1]))

    for r in range(rows):
        c1, c2 = copies(r)
        _start_row_copy(c1, 0)
        _start_row_copy(c2, 1)
    pltpu.make_async_copy(ys_hbm.at[pl.ds(0, rows)], b1_ref, sem_ref.at[0]).wait()
    pltpu.make_async_copy(ys_hbm.at[pl.ds(0, rows)], b2_ref, sem_ref.at[1]).wait()

    rinfo = rinfo_ref[...]
    w1 = rinfo[:, R_W1:R_W1 + 1]
    w2 = rinfo[:, R_W2:R_W2 + 1]
    o_ref[...] = xmid_ref[...] + w1 * b1_ref[...] + w2 * b2_ref[...]


def _combine(pos3, xmid, rinfo, ys, *, tc):
    t, d = xmid.shape
    return pl.pallas_call(
        _combine_kernel,
        grid=(t // tc,),
        in_specs=[
            pl.BlockSpec((1, 1, 2 * tc), lambda i: (i, 0, 0), memory_space=pltpu.SMEM),
            pl.BlockSpec((tc, d), lambda i: (i, 0)),
            pl.BlockSpec((tc, LANES), lambda i: (i, 0)),
            pl.BlockSpec(memory_space=pl.ANY),
        ],
        out_specs=pl.BlockSpec((tc, d), lambda i: (i, 0)),
        out_shape=jax.ShapeDtypeStruct((t, d), F32),
        scratch_shapes=[pltpu.VMEM((tc, d), F32), pltpu.VMEM((tc, d), F32),
                        pltpu.SemaphoreType.DMA((2,))],
        compiler_params=_params(("arbitrary",)),
        name="moe_combine",
    )(pos3, xmid, rinfo, ys)


def _moe(x_mid, xs, rinfo, counts_row, dest, wg, wu, wd, layer):
    t = x_mid.shape[0]
    ne = wg.shape[1]
    capacity = t
    assert capacity % MOE_TILE == 0
    n_tiles = (2 * t) // MOE_TILE + ne
    dest = dest[:, 0:2, :]
    e_id = dest // capacity
    rank = dest - e_id * capacity
    counts = counts_row[0, N_GROUPS:N_GROUPS + ne].astype(jnp.int32)
    tiles_per = (counts + MOE_TILE - 1) // MOE_TILE
    tile_end = jnp.cumsum(tiles_per)
    tile_first = tile_end - tiles_per
    tile = jnp.arange(n_tiles, dtype=jnp.int32)
    tile_expert = jnp.minimum(
        jnp.sum((tile_end[None, :] <= tile[:, None]).astype(jnp.int32), axis=1), ne - 1)
    local = tile - tile_first[tile_expert]
    used = tile < tile_end[-1]
    tile_block = jnp.where(used, tile_expert * (capacity // MOE_TILE) + local, 0)
    tile_valid = jnp.where(used, jnp.clip(counts[tile_expert] - local * MOE_TILE, 0, MOE_TILE), 0)
    ys = _moe_experts(tile_block, tile_expert, tile_valid, xs, wg, wu, wd, layer)
    base = jnp.sum(jnp.where(e_id[..., None] == jnp.arange(ne, dtype=jnp.int32),
                             tile_first * MOE_TILE, 0), axis=-1)
    pos = base + rank
    steps, _, tc = dest.shape
    return _combine(pos.reshape(steps, 1, 2 * tc), x_mid, rinfo, ys, tc=tc)


def kernel(x, norm_mix_gain, w_in, a_q_gain, a_k_gain, rel_bias, b_gate_up, b_gate_bias,
           b_out_gain, proj_a, proj_b, w_out, norm_ffn_gain, w_group, b_group, w_router,
           b_router, w_gate, w_up, w_down):
    bsz, seq, d = x.shape
    depth = w_in.shape[0]
    t = bsz * seq
    a_width = proj_a.shape[1]
    a_hd = a_width // A_HEADS
    b_vw = proj_b.shape[1]
    b_qkw = b_gate_up.shape[2]
    dk, dv = b_qkw // B_HEADS, b_vw // B_HEADS
    assert seq % ATT_Q == 0 and seq % GLA_ROWS == 0 and d == 2 * a_width == 2 * b_vw
    assert b_qkw * 2 == a_width and N_GROUPS + N_EXPERTS <= LANES

    o_lr = 3 * a_width + 2 * b_qkw + 2 * b_vw
    qk_col = 3 * a_width // b_qkw
    v_col = (3 * a_width + 2 * b_qkw) // b_vw
    r_col = v_col + 1
    ga_col = (3 * a_width + 2 * b_qkw + 2 * b_vw) // d
    gb_col = ga_col + 1
    assert (3 * a_width + 2 * b_qkw + 2 * b_vw) % d == 0

    head_id = jnp.arange(a_width) // a_hd
    ones_bd = (head_id[:, None] == head_id[None, :]).astype(BF16)
    tri = jnp.tril(jnp.ones((CHUNK, CHUNK), BF16))
    bias_tile = _attn_bias_tile(rel_bias)

    x2d = x.reshape(t, d)
    tm = 512
    for l in range(depth):
        w_main = jnp.concatenate([w_in[l][:, :o_lr], w_in[l][:, o_lr + B_GATE_RANK:]],
                                 axis=1).astype(BF16)
        w_lr = jnp.pad(w_in[l][:, o_lr:o_lr + B_GATE_RANK],
                       ((0, 0), (0, LANES - B_GATE_RANK))).astype(BF16)
        u, lr = _inproj(x2d, norm_mix_gain[l][None, :], w_main, w_lr,
                        a_q_gain[l].reshape(1, a_width), a_k_gain[l].reshape(1, a_width),
                        ones_bd, tm=tm, a_width=a_width, head_dim=a_hd)
        u3 = u.reshape(bsz, seq, u.shape[1])
        y_a = _attention(u3, bias_tile, heads=A_HEADS, head_dim=a_hd)
        gup_pad = jnp.pad(b_gate_up[l], ((0, LANES - B_GATE_RANK), (0, 0)))
        y_b = _gla(u3, lr.reshape(bsz, seq, LANES), gup_pad, b_gate_bias[l][None, :],
                   b_out_gain[l].reshape(1, b_vw), tri, heads=B_HEADS, dk=dk, dv=dv,
                   qk_col=qk_col, v_col=v_col, r_col=r_col)
        w_r = jnp.pad(jnp.concatenate([w_group[l], w_router[l]], axis=1),
                      ((0, 0), (0, LANES - N_GROUPS - N_EXPERTS)))
        b_r = jnp.pad(jnp.concatenate([b_group[l], b_router[l]]),
                      (0, LANES - N_GROUPS - N_EXPERTS))[None, :]
        x_mid, rinfo, counts, dest, xs = _post(
            x2d, y_a.reshape(t, a_width), y_b.reshape(t, b_vw), u, proj_a[l].astype(BF16),
            proj_b[l].astype(BF16), w_out[l].astype(BF16), norm_ffn_gain[l][None, :], w_r, b_r,
            tm=tm, n_experts=N_EXPERTS, ga_col=ga_col, gb_col=gb_col)
        x2d = _moe(x_mid, xs, rinfo, counts, dest, w_gate, w_up, w_down, l)
    return x2d.reshape(bsz, seq, d)
```

```python
import functools

import jax
import jax.numpy as jnp
from jax import lax
from jax.experimental import pallas as pl
from jax.experimental.pallas import tpu as pltpu

F32 = jnp.float32
BF16 = jnp.bfloat16

EPS = 1e-6
CHUNK = 64
A_HEADS = 8
A_LEFT_CHUNKS = 8
REL_CLIP = 2 * CHUNK
B_HEADS = 4
B_GATE_RANK = 16
B_GATE_TEMP = 16.0
N_GROUPS = 4
EXPERTS_PER_GROUP = 8
N_EXPERTS = N_GROUPS * EXPERTS_PER_GROUP

LANES = 128
LOG2E = 1.4426950408889634
MASK_NEG = -1e30
VMEM_LIMIT = 56 * 1024 * 1024

ATT_Q = 256
ATT_K = ATT_Q + A_LEFT_CHUNKS * CHUNK
GLA_ROWS = 256
SUB = 16


def _params(sem):
    return pltpu.CompilerParams(dimension_semantics=sem, vmem_limit_bytes=VMEM_LIMIT)


def _const_spec(shape):
    nd = len(shape)
    return pl.BlockSpec(shape, lambda *_: (0,) * nd, pipeline_mode=pl.Buffered(1))


def _dot(a, b):
    return jnp.dot(a, b, preferred_element_type=F32)


def _dot_nt(a, b):
    return lax.dot_general(a, b, (((1,), (1,)), ((), ())), preferred_element_type=F32)


def _dot_tn(a, b):
    return lax.dot_general(a, b, (((0,), (0,)), ((), ())), preferred_element_type=F32)


def _split_bf16(w):
    hi = w.astype(BF16)
    return hi, (w - hi.astype(F32)).astype(BF16)


def _dot_split(x, w_hi, w_lo):
    x_hi, x_lo = _split_bf16(x)
    return _dot(x_hi, w_hi) + _dot(x_lo, w_hi) + _dot(x_hi, w_lo)


def _rms(x, gain):
    return x * lax.rsqrt(jnp.mean(x * x, axis=-1, keepdims=True) + EPS) * gain


def _head_norm(acc, ones_bd, gain, head_dim, scale):
    ss = _dot((acc * acc).astype(BF16), ones_bd)
    inv = lax.rsqrt(ss * (1.0 / head_dim) + EPS)
    return acc * inv * (gain * scale)


def _inproj_kernel(x_ref, gain_ref, w_ref, wlr_ref, qg_ref, kg_ref, ones_ref,
                   u_ref, lr_ref, *, a_width, head_dim, col_chunk):
    x = x_ref[...]
    h = _rms(x, gain_ref[...]).astype(BF16)
    n_cols = w_ref.shape[1]
    for c0 in range(0, n_cols, col_chunk):
        acc = _dot(h, w_ref[:, c0:c0 + col_chunk])
        if c0 == 0:
            acc = _head_norm(acc, ones_ref[...], qg_ref[...], head_dim, head_dim ** -0.5 * LOG2E)
        elif c0 == a_width:
            acc = _head_norm(acc, ones_ref[...], kg_ref[...], head_dim, 1.0)
        u_ref[:, c0:c0 + col_chunk] = acc.astype(BF16)
    lr_ref[...] = _dot(h, wlr_ref[...])


def _inproj(x2d, gain, w_main, w_lr, q_gain, k_gain, ones_bd, *, tm, a_width, head_dim):
    t, d = x2d.shape
    n = w_main.shape[1]
    kern = functools.partial(_inproj_kernel, a_width=a_width, head_dim=head_dim,
                             col_chunk=a_width)
    return pl.pallas_call(
        kern,
        grid=(t // tm,),
        in_specs=[
            pl.BlockSpec((tm, d), lambda i: (i, 0)),
            _const_spec((1, d)),
            _const_spec((d, n)),
            _const_spec((d, LANES)),
            _const_spec((1, a_width)),
            _const_spec((1, a_width)),
            _const_spec((a_width, a_width)),
        ],
        out_specs=[
            pl.BlockSpec((tm, n), lambda i: (i, 0)),
            pl.BlockSpec((tm, LANES), lambda i: (i, 0)),
        ],
        out_shape=[
            jax.ShapeDtypeStruct((t, n), BF16),
            jax.ShapeDtypeStruct((t, LANES), F32),
        ],
        compiler_params=_params(("parallel",)),
        name="inproj",
    )(x2d, gain, w_main, w_lr, q_gain, k_gain, ones_bd)


def _attn_kernel(q_ref, k0_ref, k1_ref, k2_ref, v0_ref, v1_ref, v2_ref, bias_ref, o_ref,
                 *, heads, head_dim):
    i = pl.program_id(1)
    k_refs = (k0_ref, k1_ref, k2_ref)
    v_refs = (v0_ref, v1_ref, v2_ref)
    nq = q_ref.shape[0]
    ones = jnp.ones((nq, head_dim), BF16)

    def body(first_valid):
        for h in range(heads):
            hs = slice(h * head_dim, (h + 1) * head_dim)
            q = q_ref[:, hs]
            s = [_dot_nt(q, k_refs[j][:, hs]) + bias_ref[h, :, j * nq:(j + 1) * nq]
                 for j in range(first_valid, 3)]
            m = jnp.max(s[0], axis=-1, keepdims=True)
            for sj in s[1:]:
                m = jnp.maximum(m, jnp.max(sj, axis=-1, keepdims=True))
            o = None
            for j, sj in zip(range(first_valid, 3), s):
                p = jnp.exp2(sj - m).astype(BF16)
                v_aug = jnp.concatenate([v_refs[j][:, hs], ones], axis=1)
                oj = _dot(p, v_aug)
                o = oj if o is None else o + oj
            o_ref[:, hs] = (o[:, :head_dim] / o[:, head_dim:head_dim + 1]).astype(BF16)

    for fv in range(3):
        @pl.when(jnp.maximum(2 - i, 0) == fv)
        def _(fv=fv):
            body(fv)


def _attention(u3, bias_tile, *, heads, head_dim):
    b, s, _ = u3.shape
    width = heads * head_dim
    nblk = s // ATT_Q
    kern = functools.partial(_attn_kernel, heads=heads, head_dim=head_dim)

    def kv_spec(col, back):
        return pl.BlockSpec((None, ATT_Q, width),
                            lambda bi, i: (bi, jnp.maximum(i - back, 0), col))

    return pl.pallas_call(
        kern,
        grid=(b, nblk),
        in_specs=[
            pl.BlockSpec((None, ATT_Q, width), lambda bi, i: (bi, i, 0)),
            kv_spec(1, 2), kv_spec(1, 1), kv_spec(1, 0),
            kv_spec(2, 2), kv_spec(2, 1), kv_spec(2, 0),
            _const_spec(bias_tile.shape),
        ],
        out_specs=pl.BlockSpec((None, ATT_Q, width), lambda bi, i: (bi, i, 0)),
        out_shape=jax.ShapeDtypeStruct((b, s, width), BF16),
        compiler_params=_params(("parallel", "parallel")),
        name="band_attn",
    )(u3, u3, u3, u3, u3, u3, u3, bias_tile)


def _attn_bias_tile(rel_bias):
    pad = A_LEFT_CHUNKS * CHUNK
    r = jnp.arange(ATT_Q)[:, None]
    c = jnp.arange(ATT_K)[None, :]
    heads = rel_bias.shape[1]
    rb = rel_bias.astype(F32).T
    hi, lo = rb[:, -1:], rb[:, :1]
    g = jnp.concatenate([
        jnp.broadcast_to(hi, (heads, pad - REL_CLIP + 1)),
        rb[:, 2 * REL_CLIP - 1:0:-1],
        jnp.broadcast_to(lo, (heads, ATT_K - pad - REL_CLIP)),
        lo,
        jnp.broadcast_to(hi, (heads, ATT_Q - 1)),
    ], axis=1)
    length = ATT_Q + ATT_K
    skew = jnp.broadcast_to(g[:, None, :], (heads, ATT_Q, length)).reshape(heads, -1)
    skew = skew[:, :ATT_Q * (length - 1)].reshape(heads, ATT_Q, length - 1)
    bias = skew[:, :, :ATT_K]
    qc, kc = r // CHUNK, c // CHUNK
    allowed = (kc >= qc) & (kc <= qc + A_LEFT_CHUNKS)
    return jnp.where(allowed[None], bias * LOG2E, MASK_NEG)


def _log_sigmoid(z):
    return jnp.minimum(z, 0.0) - jnp.log(1.0 + jnp.exp(-jnp.abs(z)))


def _gla_kernel(q_ref, k_ref, v_ref, r_ref, lr_ref, guph_ref, gupl_ref, gbias_ref, ogain_ref, tri_ref,
                y_ref, state_ref, *, heads, dk, dv):
    @pl.when(pl.program_id(1) == 0)
    def _():
        state_ref[...] = jnp.zeros_like(state_ref)

    rows = q_ref.shape[0]
    qk_w = heads * dk
    row_id = lax.broadcasted_iota(jnp.int32, (CHUNK, qk_w), 0)
    ci = lax.broadcasted_iota(jnp.int32, (CHUNK, CHUNK), 0)
    cj = lax.broadcasted_iota(jnp.int32, (CHUNK, CHUNK), 1)
    causal = ci >= cj
    n_sub = CHUNK // SUB

    for c in range(rows // CHUNK):
        rs = slice(c * CHUNK, (c + 1) * CHUNK)
        z = _dot_split(lr_ref[rs, :], guph_ref[...], gupl_ref[...]) + gbias_ref[...]
        log_a = _log_sigmoid(z) * (1.0 / B_GATE_TEMP)
        la_hi = log_a.astype(BF16)
        la_r1 = log_a - la_hi.astype(F32)
        la_mid = la_r1.astype(BF16)
        la_lo = (la_r1 - la_mid.astype(F32)).astype(BF16)
        b = _dot(tri_ref[...], la_hi) + _dot(tri_ref[...], la_mid) + _dot(tri_ref[...], la_lo)
        q = q_ref[rs, :].astype(F32) * (dk ** -0.5)
        k = k_ref[rs, :].astype(F32)
        b_last = b[CHUNK - 1:CHUNK, :]

        ref_rows = [b[(j + 1) * SUB - 1:(j + 1) * SUB, :] for j in range(n_sub)]
        ref_full = jnp.concatenate(
            [jnp.broadcast_to(rj, (SUB, qk_w)) for rj in ref_rows], axis=0)
        k_dec = k * jnp.exp(ref_full - b)
        q_dec = [q * jnp.exp(jnp.where(row_id >= j * SUB, b - ref_rows[j], -jnp.inf))
                 for j in range(n_sub)]
        k_sub = [jnp.where((row_id >= j * SUB) & (row_id < (j + 1) * SUB), k_dec, 0.0)
                 for j in range(n_sub)]
        q_in = (q * jnp.exp(b)).astype(BF16)
        k_out = (k * jnp.exp(b_last - b)).astype(BF16)
        e_last = jnp.exp(b_last)

        for h in range(heads):
            ks = slice(h * dk, (h + 1) * dk)
            vs = slice(h * dv, (h + 1) * dv)
            q_cat = jnp.concatenate([qd[:, ks] for qd in q_dec], axis=1).astype(BF16)
            k_cat = jnp.concatenate([kz[:, ks] for kz in k_sub], axis=1).astype(BF16)
            attn = jnp.where(causal, _dot_nt(q_cat, k_cat), 0.0)
            v_h = v_ref[rs, vs]
            st = state_ref[h]
            o = _dot(attn.astype(BF16), v_h) + _dot_nt(q_in[:, ks], st.astype(BF16))
            state_ref[h] = st * e_last[:, ks] + _dot_tn(v_h, k_out[:, ks])
            r_h = r_ref[rs, vs].astype(F32)
            y = _rms(o, ogain_ref[:, vs]) * (r_h * jax.nn.sigmoid(r_h))
            y_ref[rs, vs] = y.astype(BF16)


def _gla(u3, lr3, gup_pad, gbias, ogain, tri, *, heads, dk, dv, qk_col, v_col, r_col):
    b, s, _ = u3.shape
    qk_w, v_w = heads * dk, heads * dv
    kern = functools.partial(_gla_kernel, heads=heads, dk=dk, dv=dv)
    return pl.pallas_call(
        kern,
        grid=(b, s // GLA_ROWS),
        in_specs=[
            pl.BlockSpec((None, GLA_ROWS, qk_w), lambda bi, i: (bi, i, qk_col)),
            pl.BlockSpec((None, GLA_ROWS, qk_w), lambda bi, i: (bi, i, qk_col + 1)),
            pl.BlockSpec((None, GLA_ROWS, v_w), lambda bi, i: (bi, i, v_col)),
            pl.BlockSpec((None, GLA_ROWS, v_w), lambda bi, i: (bi, i, r_col)),
            pl.BlockSpec((None, GLA_ROWS, LANES), lambda bi, i: (bi, i, 0)),
            _const_spec(gup_pad.shape),
            _const_spec(gup_pad.shape),
            _const_spec(gbias.shape),
            _const_spec(ogain.shape),
            _const_spec(tri.shape),
        ],
        out_specs=pl.BlockSpec((None, GLA_ROWS, v_w), lambda bi, i: (bi, i, 0)),
        out_shape=jax.ShapeDtypeStruct((b, s, v_w), BF16),
        scratch_shapes=[pltpu.VMEM((heads, dv, dk), F32)],
        compiler_params=_params(("parallel", "arbitrary")),
        name="gla",
    )(u3, u3, u3, u3, lr3, *_split_bf16(gup_pad), gbias, ogain, tri)


def _route(logits):
    lane = lax.broadcasted_iota(jnp.int32, logits.shape, 1)
    neg_inf = -jnp.inf
    gl = jnp.where(lane < N_GROUPS, logits, neg_inf)
    gmax = jnp.max(gl, axis=-1, keepdims=True)
    g_idx = jnp.min(jnp.where(gl == gmax, lane, LANES), axis=-1, keepdims=True)
    g_top = 1.0 / jnp.sum(jnp.exp(gl - gmax), axis=-1, keepdims=True)
    e_lane = lane - N_GROUPS
    in_group = (e_lane >= g_idx * EXPERTS_PER_GROUP) & (e_lane < (g_idx + 1) * EXPERTS_PER_GROUP)
    el = jnp.where(in_group, logits, neg_inf)
    e1 = jnp.max(el, axis=-1, keepdims=True)
    i1 = jnp.min(jnp.where(el == e1, lane, LANES), axis=-1, keepdims=True)
    el2 = jnp.where(lane == i1, neg_inf, el)
    e2 = jnp.max(el2, axis=-1, keepdims=True)
    i2 = jnp.min(jnp.where(el2 == e2, lane, LANES), axis=-1, keepdims=True)
    t = jnp.exp(e2 - e1)
    w1 = g_top / (1.0 + t)
    w2 = g_top * t / (1.0 + t)
    return lane, i1, i2, w1, w2


R_E1, R_E2, R_W1, R_W2, R_RANK1, R_RANK2 = range(6)


def _row_copy(src, row, dst, dst_row, sem):
    return pltpu.make_async_copy(src.at[pl.ds(row, 1)], dst.at[pl.ds(dst_row, 1)], sem)


def _post_kernel(x_ref, ya_ref, yb_ref, ga_ref, gb_ref, pa_ref, pb_ref, wo_ref, gain_ref,
                 wrh_ref, wrl_ref, br_ref, tri_ref, xmid_ref, rinfo_ref, counts_ref, dest_ref,
                 xs_hbm, h_scr, idx_vmem, idx_smem, row_sem, idx_sem, *, capacity, dump_row):
    i = pl.program_id(0)
    n_steps = pl.num_programs(0)
    slot = i % 2
    rows = x_ref.shape[0]

    def dispatch(s):
        for r in range(rows):
            for c in range(2):
                _row_copy(h_scr.at[s], r, xs_hbm, idx_smem[s, c, r], row_sem.at[c]).start(priority=c)

    def wait_dispatch(s):
        for c in range(2):
            pltpu.make_async_copy(h_scr.at[s], xs_hbm.at[pl.ds(0, rows)], row_sem.at[c]).wait()

    @pl.when(i == 0)
    def _():
        counts_ref[...] = jnp.zeros_like(counts_ref)
        h_scr[1] = jnp.zeros(h_scr.shape[1:], F32)

        def fill(r, carry):
            idx_smem[1, 0, r] = dump_row + r
            idx_smem[1, 1, r] = dump_row + rows + r
            return carry
        lax.fori_loop(0, rows, fill, 0)

    dispatch(1 - slot)

    ga = jax.nn.sigmoid(ga_ref[...].astype(F32))
    gb = jax.nn.sigmoid(gb_ref[...].astype(F32))
    merged = ga * _dot(ya_ref[...], pa_ref[...]) + gb * _dot(yb_ref[...], pb_ref[...])
    x_mid = x_ref[...] + _dot(merged.astype(BF16), wo_ref[...])
    xmid_ref[...] = x_mid
    h = _rms(x_mid, gain_ref[...])
    h_scr[slot] = h
    logits = _dot_split(h, wrh_ref[...], wrl_ref[...]) + br_ref[...]
    lane, i1, i2, w1, w2 = _route(logits)

    onehot = jnp.where((lane == i1) | (lane == i2), 1.0, 0.0)
    total = _dot(tri_ref[...], onehot.astype(BF16)) + counts_ref[...]
    rank1 = jnp.sum(jnp.where(lane == i1, total, 0.0), axis=-1, keepdims=True)
    rank2 = jnp.sum(jnp.where(lane == i2, total, 0.0), axis=-1, keepdims=True)
    counts_ref[...] += jnp.sum(onehot, axis=0, keepdims=True)

    rec = jnp.zeros(logits.shape, F32)
    for lane_id, val in ((R_E1, i1.astype(F32)), (R_E2, i2.astype(F32)), (R_W1, w1), (R_W2, w2),
                         (R_RANK1, rank1), (R_RANK2, rank2)):
        rec = jnp.where(lane == lane_id, val, rec)
    rinfo_ref[...] = rec

    dest1 = (i1 - N_GROUPS) * capacity + rank1.astype(jnp.int32)
    dest2 = (i2 - N_GROUPS) * capacity + rank2.astype(jnp.int32)
    dest = jnp.where(lane == 0, dest1, jnp.where(lane == 1, dest2, 0))
    dest_rows = jnp.transpose(dest, (1, 0))[0:idx_vmem.shape[0], :]
    idx_vmem[...] = dest_rows
    dest_ref[0] = dest_rows
    idx_copy = pltpu.make_async_copy(idx_vmem, idx_smem.at[slot], idx_sem)
    idx_copy.start()
    idx_copy.wait()

    wait_dispatch(1 - slot)

    @pl.when(i == n_steps - 1)
    def _():
        dispatch(slot)
        wait_dispatch(slot)


def _post(x2d, ya, yb, u2d, pa, pb, wo, gain, wr, br, *, tm, n_experts, ga_col, gb_col):
    t, d = x2d.shape
    aw, bw = ya.shape[1], yb.shape[1]
    wr_hi, wr_lo = _split_bf16(wr)
    tri = jnp.tril(jnp.ones((tm, tm), BF16), k=-1)
    capacity = t
    dump_row = n_experts * capacity
    sublanes = 8
    kern = functools.partial(_post_kernel, capacity=capacity, dump_row=dump_row)
    return pl.pallas_call(
        kern,
        grid=(t // tm,),
        in_specs=[
            pl.BlockSpec((tm, d), lambda i: (i, 0)),
            pl.BlockSpec((tm, aw), lambda i: (i, 0)),
            pl.BlockSpec((tm, bw), lambda i: (i, 0)),
            pl.BlockSpec((tm, d), lambda i: (i, ga_col)),
            pl.BlockSpec((tm, d), lambda i: (i, gb_col)),
            _const_spec(pa.shape), _const_spec(pb.shape), _const_spec(wo.shape),
            _const_spec(gain.shape), _const_spec(wr.shape), _const_spec(wr.shape),
            _const_spec(br.shape), _const_spec(tri.shape),
        ],
        out_specs=[
            pl.BlockSpec((tm, d), lambda i: (i, 0)),
            pl.BlockSpec((tm, LANES), lambda i: (i, 0)),
            pl.BlockSpec((1, LANES), lambda i: (0, 0)),
            pl.BlockSpec((1, sublanes, tm), lambda i: (i, 0, 0)),
            pl.BlockSpec(memory_space=pl.ANY),
        ],
        out_shape=[
            jax.ShapeDtypeStruct((t, d), F32),
            jax.ShapeDtypeStruct((t, LANES), F32),
            jax.ShapeDtypeStruct((1, LANES), F32),
            jax.ShapeDtypeStruct((t // tm, sublanes, tm), jnp.int32),
            jax.ShapeDtypeStruct((dump_row + 2 * tm, d), F32),
        ],
        scratch_shapes=[
            pltpu.VMEM((2, tm, d), F32),
            pltpu.VMEM((sublanes, tm), jnp.int32),
            pltpu.SMEM((2, sublanes, tm), jnp.int32),
            pltpu.SemaphoreType.DMA((2,)),
            pltpu.SemaphoreType.DMA(()),
        ],
        compiler_params=_params(("arbitrary",)),
        name="merge_route",
    )(x2d, ya, yb, u2d, u2d, pa, pb, wo, gain, wr_hi, wr_lo, br, tri)


MOE_TILE = 512


def _start_row_copy(copy, index):
    copy.start(priority=index % 2)


def _moe_kernel(tile_block_ref, tile_expert_ref, tile_valid_ref, x_ref, wg_ref, wu_ref, wd_ref,
                ys_ref):
    del tile_block_ref, tile_expert_ref
    valid = tile_valid_ref[pl.program_id(0)]
    row = lax.broadcasted_iota(jnp.int32, (x_ref.shape[0], 1), 0)
    x = jnp.where(row < valid, x_ref[...], 0.0).astype(BF16)
    a = _dot(x, wg_ref[...].astype(BF16))
    a = a * jax.nn.sigmoid(a) * _dot(x, wu_ref[...].astype(BF16))
    ys_ref[...] = _dot(a.astype(BF16), wd_ref[...].astype(BF16))


def _moe_experts(tile_block, tile_expert, tile_valid, xs, wg, wu, wd, layer):
    n_tiles = tile_block.shape[0]
    _, d = xs.shape
    _, _, _, de = wg.shape
    grid_spec = pltpu.PrefetchScalarGridSpec(
        num_scalar_prefetch=3,
        grid=(n_tiles,),
        in_specs=[
            pl.BlockSpec((MOE_TILE, d), lambda i, tb, te, tv: (tb[i], 0)),
            pl.BlockSpec((None, None, d, de), lambda i, tb, te, tv: (layer, te[i], 0, 0)),
            pl.BlockSpec((None, None, d, de), lambda i, tb, te, tv: (layer, te[i], 0, 0)),
            pl.BlockSpec((None, None, de, d), lambda i, tb, te, tv: (layer, te[i], 0, 0)),
        ],
        out_specs=pl.BlockSpec((MOE_TILE, d), lambda i, tb, te, tv: (i, 0)),
    )
    return pl.pallas_call(
        _moe_kernel,
        grid_spec=grid_spec,
        out_shape=jax.ShapeDtypeStruct((n_tiles * MOE_TILE, d), F32),
        compiler_params=_params(("arbitrary",)),
        name="moe_experts",
    )(tile_block, tile_expert, tile_valid, xs, wg, wu, wd)


def _combine_kernel(pos_ref, xmid_ref, rinfo_ref, ys_hbm, o_ref, b1_ref, b2_ref, sem_ref):
    rows = o_ref.shape[0]

    def copies(r):
        return (_row_copy(ys_hbm, pos_ref[0, 0, r], b1_ref, r, sem_ref.at[0]),
                _row_copy(ys_hbm, pos_ref[0, 0, rows + r], b2_ref, r, sem_ref.at[1]))

    for r in range(rows):
        c1, c2 = copies(r)
        _start_row_copy(c1, 0)
        _start_row_copy(c2, 1)
    pltpu.make_async_copy(ys_hbm.at[pl.ds(0, rows)], b1_ref, sem_ref.at[0]).wait()
    pltpu.make_async_copy(ys_hbm.at[pl.ds(0, rows)], b2_ref, sem_ref.at[1]).wait()

    rinfo = rinfo_ref[...]
    w1 = rinfo[:, R_W1:R_W1 + 1]
    w2 = rinfo[:, R_W2:R_W2 + 1]
    o_ref[...] = xmid_ref[...] + w1 * b1_ref[...] + w2 * b2_ref[...]


def _combine(pos3, xmid, rinfo, ys, *, tc):
    t, d = xmid.shape
    return pl.pallas_call(
        _combine_kernel,
        grid=(t // tc,),
        in_specs=[
            pl.BlockSpec((1, 1, 2 * tc), lambda i: (i, 0, 0), memory_space=pltpu.SMEM),
            pl.BlockSpec((tc, d), lambda i: (i, 0)),
            pl.BlockSpec((tc, LANES), lambda i: (i, 0)),
            pl.BlockSpec(memory_space=pl.ANY),
        ],
        out_specs=pl.BlockSpec((tc, d), lambda i: (i, 0)),
        out_shape=jax.ShapeDtypeStruct((t, d), F32),
        scratch_shapes=[pltpu.VMEM((tc, d), F32), pltpu.VMEM((tc, d), F32),
                        pltpu.SemaphoreType.DMA((2,))],
        compiler_params=_params(("arbitrary",)),
        name="moe_combine",
    )(pos3, xmid, rinfo, ys)


def _moe(x_mid, xs, rinfo, counts_row, dest, wg, wu, wd, layer):
    t = x_mid.shape[0]
    ne = wg.shape[1]
    capacity = t
    assert capacity % MOE_TILE == 0
    n_tiles = (2 * t) // MOE_TILE + ne
    dest = dest[:, 0:2, :]
    e_id = dest // capacity
    rank = dest - e_id * capacity
    counts = counts_row[0, N_GROUPS:N_GROUPS + ne].astype(jnp.int32)
    tiles_per = (counts + MOE_TILE - 1) // MOE_TILE
    tile_end = jnp.cumsum(tiles_per)
    tile_first = tile_end - tiles_per
    tile = jnp.arange(n_tiles, dtype=jnp.int32)
    tile_expert = jnp.minimum(
        jnp.sum((tile_end[None, :] <= tile[:, None]).astype(jnp.int32), axis=1), ne - 1)
    local = tile - tile_first[tile_expert]
    used = tile < tile_end[-1]
    tile_block = jnp.where(used, tile_expert * (capacity // MOE_TILE) + local, 0)
    tile_valid = jnp.where(used, jnp.clip(counts[tile_expert] - local * MOE_TILE, 0, MOE_TILE), 0)
    ys = _moe_experts(tile_block, tile_expert, tile_valid, xs, wg, wu, wd, layer)
    base = jnp.sum(jnp.where(e_id[..., None] == jnp.arange(ne, dtype=jnp.int32),
                             tile_first * MOE_TILE, 0), axis=-1)
    pos = base + rank
    steps, _, tc = dest.shape
    return _combine(pos.reshape(steps, 1, 2 * tc), x_mid, rinfo, ys, tc=tc)


def kernel(x, norm_mix_gain, w_in, a_q_gain, a_k_gain, rel_bias, b_gate_up, b_gate_bias,
           b_out_gain, proj_a, proj_b, w_out, norm_ffn_gain, w_group, b_group, w_router,
           b_router, w_gate, w_up, w_down):
    bsz, seq, d = x.shape
    depth = w_in.shape[0]
    t = bsz * seq
    a_width = proj_a.shape[1]
    a_hd = a_width // A_HEADS
    b_vw = proj_b.shape[1]
    b_qkw = b_gate_up.shape[2]
    dk, dv = b_qkw // B_HEADS, b_vw // B_HEADS
    assert seq % ATT_Q == 0 and seq % GLA_ROWS == 0 and d == 2 * a_width == 2 * b_vw
    assert b_qkw * 2 == a_width and N_GROUPS + N_EXPERTS <= LANES

    o_lr = 3 * a_width + 2 * b_qkw + 2 * b_vw
    qk_col = 3 * a_width // b_qkw
    v_col = (3 * a_width + 2 * b_qkw) // b_vw
    r_col = v_col + 1
    ga_col = (3 * a_width + 2 * b_qkw + 2 * b_vw) // d
    gb_col = ga_col + 1
    assert (3 * a_width + 2 * b_qkw + 2 * b_vw) % d == 0

    head_id = jnp.arange(a_width) // a_hd
    ones_bd = (head_id[:, None] == head_id[None, :]).astype(BF16)
    tri = jnp.tril(jnp.ones((CHUNK, CHUNK), BF16))
    bias_tile = _attn_bias_tile(rel_bias)

    x2d = x.reshape(t, d)
    tm = 512
    for l in range(depth):
        w_main = jnp.concatenate([w_in[l][:, :o_lr], w_in[l][:, o_lr + B_GATE_RANK:]],
                                 axis=1).astype(BF16)
        w_lr = jnp.pad(w_in[l][:, o_lr:o_lr + B_GATE_RANK],
                       ((0, 0), (0, LANES - B_GATE_RANK))).astype(BF16)
        u, lr = _inproj(x2d, norm_mix_gain[l][None, :], w_main, w_lr,
                        a_q_gain[l].reshape(1, a_width), a_k_gain[l].reshape(1, a_width),
                        ones_bd, tm=tm, a_width=a_width, head_dim=a_hd)
        u3 = u.reshape(bsz, seq, u.shape[1])
        y_a = _attention(u3, bias_tile, heads=A_HEADS, head_dim=a_hd)
        gup_pad = jnp.pad(b_gate_up[l], ((0, LANES - B_GATE_RANK), (0, 0)))
        y_b = _gla(u3, lr.reshape(bsz, seq, LANES), gup_pad, b_gate_bias[l][None, :],
                   b_out_gain[l].reshape(1, b_vw), tri, heads=B_HEADS, dk=dk, dv=dv,
                   qk_col=qk_col, v_col=v_col, r_col=r_col)
        w_r = jnp.pad(jnp.concatenate([w_group[l], w_router[l]], axis=1),
                      ((0, 0), (0, LANES - N_GROUPS - N_EXPERTS)))
        b_r = jnp.pad(jnp.concatenate([b_group[l], b_router[l]]),
                      (0, LANES - N_GROUPS - N_EXPERTS))[None, :]
        x_mid, rinfo, counts, dest, xs = _post(
            x2d, y_a.reshape(t, a_width), y_b.reshape(t, b_vw), u, proj_a[l].astype(BF16),
            proj_b[l].astype(BF16), w_out[l].astype(BF16), norm_ffn_gain[l][None, :], w_r, b_r,
            tm=tm, n_experts=N_EXPERTS, ga_col=ga_col, gb_col=gb_col)
        x2d = _moe(x_mid, xs, rinfo, counts, dest, w_gate, w_up, w_down, l)
    return x2d.reshape(bsz, seq, d)
```

```python
import functools

import jax
import jax.numpy as jnp
from jax import lax
from jax.experimental import pallas as pl
from jax.experimental.pallas import tpu as pltpu

F32 = jnp.float32
BF16 = jnp.bfloat16

EPS = 1e-6
CHUNK = 64
A_HEADS = 8
A_LEFT_CHUNKS = 8
REL_CLIP = 2 * CHUNK
B_HEADS = 4
B_GATE_RANK = 16
B_GATE_TEMP = 16.0
N_GROUPS = 4
EXPERTS_PER_GROUP = 8
N_EXPERTS = N_GROUPS * EXPERTS_PER_GROUP

LANES = 128
LOG2E = 1.4426950408889634
MASK_NEG = -1e30
VMEM_LIMIT = 56 * 1024 * 1024

ATT_Q = 256
ATT_K = ATT_Q + A_LEFT_CHUNKS * CHUNK
GLA_ROWS = 512
SUB = 16


def _params(sem):
    return pltpu.CompilerParams(dimension_semantics=sem, vmem_limit_bytes=VMEM_LIMIT)


def _const_spec(shape):
    nd = len(shape)
    return pl.BlockSpec(shape, lambda *_: (0,) * nd, pipeline_mode=pl.Buffered(1))


def _dot(a, b):
    return jnp.dot(a, b, preferred_element_type=F32)


def _dot_nt(a, b):
    return lax.dot_general(a, b, (((1,), (1,)), ((), ())), preferred_element_type=F32)


def _dot_tn(a, b):
    return lax.dot_general(a, b, (((0,), (0,)), ((), ())), preferred_element_type=F32)


def _split_bf16(w):
    hi = w.astype(BF16)
    return hi, (w - hi.astype(F32)).astype(BF16)


def _dot_split(x, w_hi, w_lo):
    x_hi, x_lo = _split_bf16(x)
    return _dot(x_hi, w_hi) + _dot(x_lo, w_hi) + _dot(x_hi, w_lo)


def _rms(x, gain):
    return x * lax.rsqrt(jnp.mean(x * x, axis=-1, keepdims=True) + EPS) * gain


def _head_norm(acc, ones_bd, gain, head_dim, scale):
    ss = _dot((acc * acc).astype(BF16), ones_bd)
    inv = lax.rsqrt(ss * (1.0 / head_dim) + EPS)
    return acc * inv * (gain * scale)


def _inproj_kernel(x_ref, gain_ref, w_ref, wlr_ref, qg_ref, kg_ref, ones_ref,
                   u_ref, lr_ref, *, a_width, head_dim, col_chunk):
    x = x_ref[...]
    h = _rms(x, gain_ref[...]).astype(BF16)
    n_cols = w_ref.shape[1]
    for c0 in range(0, n_cols, col_chunk):
        acc = _dot(h, w_ref[:, c0:c0 + col_chunk])
        if c0 == 0:
            acc = _head_norm(acc, ones_ref[...], qg_ref[...], head_dim, head_dim ** -0.5 * LOG2E)
        elif c0 == a_width:
            acc = _head_norm(acc, ones_ref[...], kg_ref[...], head_dim, 1.0)
        u_ref[:, c0:c0 + col_chunk] = acc.astype(BF16)
    lr_ref[...] = _dot(h, wlr_ref[...])


def _inproj(x2d, gain, w_main, w_lr, q_gain, k_gain, ones_bd, *, tm, a_width, head_dim):
    t, d = x2d.shape
    n = w_main.shape[1]
    kern = functools.partial(_inproj_kernel, a_width=a_width, head_dim=head_dim,
                             col_chunk=a_width)
    return pl.pallas_call(
        kern,
        grid=(t // tm,),
        in_specs=[
            pl.BlockSpec((tm, d), lambda i: (i, 0)),
            _const_spec((1, d)),
            _const_spec((d, n)),
            _const_spec((d, LANES)),
            _const_spec((1, a_width)),
            _const_spec((1, a_width)),
            _const_spec((a_width, a_width)),
        ],
        out_specs=[
            pl.BlockSpec((tm, n), lambda i: (i, 0)),
            pl.BlockSpec((tm, LANES), lambda i: (i, 0)),
        ],
        out_shape=[
            jax.ShapeDtypeStruct((t, n), BF16),
            jax.ShapeDtypeStruct((t, LANES), F32),
        ],
        compiler_params=_params(("parallel",)),
        name="inproj",
    )(x2d, gain, w_main, w_lr, q_gain, k_gain, ones_bd)


def _attn_kernel(q_ref, k0_ref, k1_ref, k2_ref, v0_ref, v1_ref, v2_ref, bias_ref, o_ref,
                 *, heads, head_dim):
    i = pl.program_id(1)
    k_refs = (k0_ref, k1_ref, k2_ref)
    v_refs = (v0_ref, v1_ref, v2_ref)
    nq = q_ref.shape[0]
    ones = jnp.ones((nq, head_dim), BF16)

    def body(first_valid):
        for h in range(heads):
            hs = slice(h * head_dim, (h + 1) * head_dim)
            q = q_ref[:, hs]
            s = [_dot_nt(q, k_refs[j][:, hs]) + bias_ref[h, :, j * nq:(j + 1) * nq]
                 for j in range(first_valid, 3)]
            m = jnp.max(s[0], axis=-1, keepdims=True)
            for sj in s[1:]:
                m = jnp.maximum(m, jnp.max(sj, axis=-1, keepdims=True))
            o = None
            for j, sj in zip(range(first_valid, 3), s):
                p = jnp.exp2(sj - m).astype(BF16)
                v_aug = jnp.concatenate([v_refs[j][:, hs], ones], axis=1)
                oj = _dot(p, v_aug)
                o = oj if o is None else o + oj
            o_ref[:, hs] = (o[:, :head_dim] / o[:, head_dim:head_dim + 1]).astype(BF16)

    for fv in range(3):
        @pl.when(jnp.maximum(2 - i, 0) == fv)
        def _(fv=fv):
            body(fv)


def _attention(u3, bias_tile, *, heads, head_dim):
    b, s, _ = u3.shape
    width = heads * head_dim
    nblk = s // ATT_Q
    kern = functools.partial(_attn_kernel, heads=heads, head_dim=head_dim)

    def kv_spec(col, back):
        return pl.BlockSpec((None, ATT_Q, width),
                            lambda bi, i: (bi, jnp.maximum(i - back, 0), col))

    return pl.pallas_call(
        kern,
        grid=(b, nblk),
        in_specs=[
            pl.BlockSpec((None, ATT_Q, width), lambda bi, i: (bi, i, 0)),
            kv_spec(1, 2), kv_spec(1, 1), kv_spec(1, 0),
            kv_spec(2, 2), kv_spec(2, 1), kv_spec(2, 0),
            _const_spec(bias_tile.shape),
        ],
        out_specs=pl.BlockSpec((None, ATT_Q, width), lambda bi, i: (bi, i, 0)),
        out_shape=jax.ShapeDtypeStruct((b, s, width), BF16),
        compiler_params=_params(("parallel", "parallel")),
        name="band_attn",
    )(u3, u3, u3, u3, u3, u3, u3, bias_tile)


def _attn_bias_tile(rel_bias):
    pad = A_LEFT_CHUNKS * CHUNK
    r = jnp.arange(ATT_Q)[:, None]
    c = jnp.arange(ATT_K)[None, :]
    heads = rel_bias.shape[1]
    rb = rel_bias.astype(F32).T
    hi, lo = rb[:, -1:], rb[:, :1]
    g = jnp.concatenate([
        jnp.broadcast_to(hi, (heads, pad - REL_CLIP + 1)),
        rb[:, 2 * REL_CLIP - 1:0:-1],
        jnp.broadcast_to(lo, (heads, ATT_K - pad - REL_CLIP)),
        lo,
        jnp.broadcast_to(hi, (heads, ATT_Q - 1)),
    ], axis=1)
    length = ATT_Q + ATT_K
    skew = jnp.broadcast_to(g[:, None, :], (heads, ATT_Q, length)).reshape(heads, -1)
    skew = skew[:, :ATT_Q * (length - 1)].reshape(heads, ATT_Q, length - 1)
    bias = skew[:, :, :ATT_K]
    qc, kc = r // CHUNK, c // CHUNK
    allowed = (kc >= qc) & (kc <= qc + A_LEFT_CHUNKS)
    return jnp.where(allowed[None], bias * LOG2E, MASK_NEG)


def _log_sigmoid(z):
    return jnp.minimum(z, 0.0) - jnp.log(1.0 + jnp.exp(-jnp.abs(z)))


def _gla_kernel(q_ref, k_ref, v_ref, r_ref, lr_ref, guph_ref, gupl_ref, gbias_ref, ogain_ref, tri_ref,
                y_ref, state_ref, *, heads, dk, dv):
    @pl.when(pl.program_id(1) == 0)
    def _():
        state_ref[...] = jnp.zeros_like(state_ref)

    rows = q_ref.shape[0]
    qk_w = heads * dk
    row_id = lax.broadcasted_iota(jnp.int32, (CHUNK, qk_w), 0)
    ci = lax.broadcasted_iota(jnp.int32, (CHUNK, CHUNK), 0)
    cj = lax.broadcasted_iota(jnp.int32, (CHUNK, CHUNK), 1)
    causal = ci >= cj
    n_sub = CHUNK // SUB

    for c in range(rows // CHUNK):
        rs = slice(c * CHUNK, (c + 1) * CHUNK)
        z = _dot_split(lr_ref[rs, :], guph_ref[...], gupl_ref[...]) + gbias_ref[...]
        log_a = _log_sigmoid(z) * (1.0 / B_GATE_TEMP)
        la_hi = log_a.astype(BF16)
        la_r1 = log_a - la_hi.astype(F32)
        la_mid = la_r1.astype(BF16)
        la_lo = (la_r1 - la_mid.astype(F32)).astype(BF16)
        b = _dot(tri_ref[...], la_hi) + _dot(tri_ref[...], la_mid) + _dot(tri_ref[...], la_lo)
        q = q_ref[rs, :].astype(F32) * (dk ** -0.5)
        k = k_ref[rs, :].astype(F32)
        b_last = b[CHUNK - 1:CHUNK, :]

        ref_rows = [b[(j + 1) * SUB - 1:(j + 1) * SUB, :] for j in range(n_sub)]
        ref_full = jnp.concatenate(
            [jnp.broadcast_to(rj, (SUB, qk_w)) for rj in ref_rows], axis=0)
        k_dec = k * jnp.exp(ref_full - b)
        q_dec = [q * jnp.exp(jnp.where(row_id >= j * SUB, b - ref_rows[j], -jnp.inf))
                 for j in range(n_sub)]
        k_sub = [jnp.where((row_id >= j * SUB) & (row_id < (j + 1) * SUB), k_dec, 0.0)
                 for j in range(n_sub)]
        q_in = (q * jnp.exp(b)).astype(BF16)
        k_out = (k * jnp.exp(b_last - b)).astype(BF16)
        e_last = jnp.exp(b_last)

        for h in range(heads):
            ks = slice(h * dk, (h + 1) * dk)
            vs = slice(h * dv, (h + 1) * dv)
            q_cat = jnp.concatenate([qd[:, ks] for qd in q_dec], axis=1).astype(BF16)
            k_cat = jnp.concatenate([kz[:, ks] for kz in k_sub], axis=1).astype(BF16)
            attn = jnp.where(causal, _dot_nt(q_cat, k_cat), 0.0)
            v_h = v_ref[rs, vs]
            st = state_ref[h]
            o = _dot(attn.astype(BF16), v_h) + _dot_nt(q_in[:, ks], st.astype(BF16))
            state_ref[h] = st * e_last[:, ks] + _dot_tn(v_h, k_out[:, ks])
            r_h = r_ref[rs, vs].astype(F32)
            y = _rms(o, ogain_ref[:, vs]) * (r_h * jax.nn.sigmoid(r_h))
            y_ref[rs, vs] = y.astype(BF16)


def _gla(u3, lr3, gup_pad, gbias, ogain, tri, *, heads, dk, dv, qk_col, v_col, r_col):
    b, s, _ = u3.shape
    qk_w, v_w = heads * dk, heads * dv
    kern = functools.partial(_gla_kernel, heads=heads, dk=dk, dv=dv)
    return pl.pallas_call(
        kern,
        grid=(b, s // GLA_ROWS),
        in_specs=[
            pl.BlockSpec((None, GLA_ROWS, qk_w), lambda bi, i: (bi, i, qk_col)),
            pl.BlockSpec((None, GLA_ROWS, qk_w), lambda bi, i: (bi, i, qk_col + 1)),
            pl.BlockSpec((None, GLA_ROWS, v_w), lambda bi, i: (bi, i, v_col)),
            pl.BlockSpec((None, GLA_ROWS, v_w), lambda bi, i: (bi, i, r_col)),
            pl.BlockSpec((None, GLA_ROWS, LANES), lambda bi, i: (bi, i, 0)),
            _const_spec(gup_pad.shape),
            _const_spec(gup_pad.shape),
            _const_spec(gbias.shape),
            _const_spec(ogain.shape),
            _const_spec(tri.shape),
        ],
        out_specs=pl.BlockSpec((None, GLA_ROWS, v_w), lambda bi, i: (bi, i, 0)),
        out_shape=jax.ShapeDtypeStruct((b, s, v_w), BF16),
        scratch_shapes=[pltpu.VMEM((heads, dv, dk), F32)],
        compiler_params=_params(("parallel", "arbitrary")),
        name="gla",
    )(u3, u3, u3, u3, lr3, *_split_bf16(gup_pad), gbias, ogain, tri)


def _route(logits):
    lane = lax.broadcasted_iota(jnp.int32, logits.shape, 1)
    neg_inf = -jnp.inf
    gl = jnp.where(lane < N_GROUPS, logits, neg_inf)
    gmax = jnp.max(gl, axis=-1, keepdims=True)
    g_idx = jnp.min(jnp.where(gl == gmax, lane, LANES), axis=-1, keepdims=True)
    g_top = 1.0 / jnp.sum(jnp.exp(gl - gmax), axis=-1, keepdims=True)
    e_lane = lane - N_GROUPS
    in_group = (e_lane >= g_idx * EXPERTS_PER_GROUP) & (e_lane < (g_idx + 1) * EXPERTS_PER_GROUP)
    el = jnp.where(in_group, logits, neg_inf)
    e1 = jnp.max(el, axis=-1, keepdims=True)
    i1 = jnp.min(jnp.where(el == e1, lane, LANES), axis=-1, keepdims=True)
    el2 = jnp.where(lane == i1, neg_inf, el)
    e2 = jnp.max(el2, axis=-1, keepdims=True)
    i2 = jnp.min(jnp.where(el2 == e2, lane, LANES), axis=-1, keepdims=True)
    t = jnp.exp(e2 - e1)
    w1 = g_top / (1.0 + t)
    w2 = g_top * t / (1.0 + t)
    return lane, i1, i2, w1, w2


R_E1, R_E2, R_W1, R_W2, R_RANK1, R_RANK2 = range(6)


def _row_copy(src, row, dst, dst_row, sem):
    return pltpu.make_async_copy(src.at[pl.ds(row, 1)], dst.at[pl.ds(dst_row, 1)], sem)


def _post_kernel(x_ref, ya_ref, yb_ref, ga_ref, gb_ref, pa_ref, pb_ref, wo_ref, gain_ref,
                 wrh_ref, wrl_ref, br_ref, tri_ref, xmid_ref, rinfo_ref, counts_ref, dest_ref,
                 xs_hbm, h_scr, idx_vmem, idx_smem, row_sem, idx_sem, *, capacity, dump_row):
    i = pl.program_id(0)
    n_steps = pl.num_programs(0)
    slot = i % 2
    rows = x_ref.shape[0]

    def dispatch(s):
        for r in range(rows):
            for c in range(2):
                _row_copy(h_scr.at[s], r, xs_hbm, idx_smem[s, c, r], row_sem.at[c]).start(priority=c)

    def wait_dispatch(s):
        for c in range(2):
            pltpu.make_async_copy(h_scr.at[s], xs_hbm.at[pl.ds(0, rows)], row_sem.at[c]).wait()

    @pl.when(i == 0)
    def _():
        counts_ref[...] = jnp.zeros_like(counts_ref)
        h_scr[1] = jnp.zeros(h_scr.shape[1:], F32)

        def fill(r, carry):
            idx_smem[1, 0, r] = dump_row + r
            idx_smem[1, 1, r] = dump_row + rows + r
            return carry
        lax.fori_loop(0, rows, fill, 0)

    dispatch(1 - slot)

    ga = jax.nn.sigmoid(ga_ref[...].astype(F32))
    gb = jax.nn.sigmoid(gb_ref[...].astype(F32))
    merged = ga * _dot(ya_ref[...], pa_ref[...]) + gb * _dot(yb_ref[...], pb_ref[...])
    x_mid = x_ref[...] + _dot(merged.astype(BF16), wo_ref[...])
    xmid_ref[...] = x_mid
    h = _rms(x_mid, gain_ref[...])
    h_scr[slot] = h
    logits = _dot_split(h, wrh_ref[...], wrl_ref[...]) + br_ref[...]
    lane, i1, i2, w1, w2 = _route(logits)

    onehot = jnp.where((lane == i1) | (lane == i2), 1.0, 0.0)
    total = _dot(tri_ref[...], onehot.astype(BF16)) + counts_ref[...]
    rank1 = jnp.sum(jnp.where(lane == i1, total, 0.0), axis=-1, keepdims=True)
    rank2 = jnp.sum(jnp.where(lane == i2, total, 0.0), axis=-1, keepdims=True)
    counts_ref[...] += jnp.sum(onehot, axis=0, keepdims=True)

    rec = jnp.zeros(logits.shape, F32)
    for lane_id, val in ((R_E1, i1.astype(F32)), (R_E2, i2.astype(F32)), (R_W1, w1), (R_W2, w2),
                         (R_RANK1, rank1), (R_RANK2, rank2)):
        rec = jnp.where(lane == lane_id, val, rec)
    rinfo_ref[...] = rec

    dest1 = (i1 - N_GROUPS) * capacity + rank1.astype(jnp.int32)
    dest2 = (i2 - N_GROUPS) * capacity + rank2.astype(jnp.int32)
    dest = jnp.where(lane == 0, dest1, jnp.where(lane == 1, dest2, 0))
    dest_rows = jnp.transpose(dest, (1, 0))[0:idx_vmem.shape[0], :]
    idx_vmem[...] = dest_rows
    dest_ref[0] = dest_rows
    idx_copy = pltpu.make_async_copy(idx_vmem, idx_smem.at[slot], idx_sem)
    idx_copy.start()
    idx_copy.wait()

    wait_dispatch(1 - slot)

    @pl.when(i == n_steps - 1)
    def _():
        dispatch(slot)
        wait_dispatch(slot)


def _post(x2d, ya, yb, u2d, pa, pb, wo, gain, wr, br, *, tm, n_experts, ga_col, gb_col):
    t, d = x2d.shape
    aw, bw = ya.shape[1], yb.shape[1]
    wr_hi, wr_lo = _split_bf16(wr)
    tri = jnp.tril(jnp.ones((tm, tm), BF16), k=-1)
    capacity = t
    dump_row = n_experts * capacity
    sublanes = 8
    kern = functools.partial(_post_kernel, capacity=capacity, dump_row=dump_row)
    return pl.pallas_call(
        kern,
        grid=(t // tm,),
        in_specs=[
            pl.BlockSpec((tm, d), lambda i: (i, 0)),
            pl.BlockSpec((tm, aw), lambda i: (i, 0)),
            pl.BlockSpec((tm, bw), lambda i: (i, 0)),
            pl.BlockSpec((tm, d), lambda i: (i, ga_col)),
            pl.BlockSpec((tm, d), lambda i: (i, gb_col)),
            _const_spec(pa.shape), _const_spec(pb.shape), _const_spec(wo.shape),
            _const_spec(gain.shape), _const_spec(wr.shape), _const_spec(wr.shape),
            _const_spec(br.shape), _const_spec(tri.shape),
        ],
        out_specs=[
            pl.BlockSpec((tm, d), lambda i: (i, 0)),
            pl.BlockSpec((tm, LANES), lambda i: (i, 0)),
            pl.BlockSpec((1, LANES), lambda i: (0, 0)),
            pl.BlockSpec((1, sublanes, tm), lambda i: (i, 0, 0)),
            pl.BlockSpec(memory_space=pl.ANY),
        ],
        out_shape=[
            jax.ShapeDtypeStruct((t, d), F32),
            jax.ShapeDtypeStruct((t, LANES), F32),
            jax.ShapeDtypeStruct((1, LANES), F32),
            jax.ShapeDtypeStruct((t // tm, sublanes, tm), jnp.int32),
            jax.ShapeDtypeStruct((dump_row + 2 * tm, d), F32),
        ],
        scratch_shapes=[
            pltpu.VMEM((2, tm, d), F32),
            pltpu.VMEM((sublanes, tm), jnp.int32),
            pltpu.SMEM((2, sublanes, tm), jnp.int32),
            pltpu.SemaphoreType.DMA((2,)),
            pltpu.SemaphoreType.DMA(()),
        ],
        compiler_params=_params(("arbitrary",)),
        name="merge_route",
    )(x2d, ya, yb, u2d, u2d, pa, pb, wo, gain, wr_hi, wr_lo, br, tri)


MOE_TILE = 512


def _start_row_copy(copy, index):
    copy.start(priority=index % 2)


def _moe_kernel(tile_block_ref, tile_expert_ref, tile_valid_ref, x_ref, wg_ref, wu_ref, wd_ref,
                ys_ref):
    del tile_block_ref, tile_expert_ref
    valid = tile_valid_ref[pl.program_id(0)]
    row = lax.broadcasted_iota(jnp.int32, (x_ref.shape[0], 1), 0)
    x = jnp.where(row < valid, x_ref[...], 0.0).astype(BF16)
    a = _dot(x, wg_ref[...].astype(BF16))
    a = a * jax.nn.sigmoid(a) * _dot(x, wu_ref[...].astype(BF16))
    ys_ref[...] = _dot(a.astype(BF16), wd_ref[...].astype(BF16))


def _moe_experts(tile_block, tile_expert, tile_valid, xs, wg, wu, wd, layer):
    n_tiles = tile_block.shape[0]
    _, d = xs.shape
    _, _, _, de = wg.shape
    grid_spec = pltpu.PrefetchScalarGridSpec(
        num_scalar_prefetch=3,
        grid=(n_tiles,),
        in_specs=[
            pl.BlockSpec((MOE_TILE, d), lambda i, tb, te, tv: (tb[i], 0)),
            pl.BlockSpec((None, None, d, de), lambda i, tb, te, tv: (layer, te[i], 0, 0)),
            pl.BlockSpec((None, None, d, de), lambda i, tb, te, tv: (layer, te[i], 0, 0)),
            pl.BlockSpec((None, None, de, d), lambda i, tb, te, tv: (layer, te[i], 0, 0)),
        ],
        out_specs=pl.BlockSpec((MOE_TILE, d), lambda i, tb, te, tv: (i, 0)),
    )
    return pl.pallas_call(
        _moe_kernel,
        grid_spec=grid_spec,
        out_shape=jax.ShapeDtypeStruct((n_tiles * MOE_TILE, d), F32),
        compiler_params=_params(("arbitrary",)),
        name="moe_experts",
    )(tile_block, tile_expert, tile_valid, xs, wg, wu, wd)


def _combine_kernel(pos_ref, xmid_ref, rinfo_ref, ys_hbm, o_ref, b1_ref, b2_ref, sem_ref):
    rows = o_ref.shape[0]

    def copies(r):
        return (_row_copy(ys_hbm, pos_ref[0, 0, r], b1_ref, r, sem_ref.at[0]),
                _row_copy(ys_hbm, pos_ref[0, 0, rows + r], b2_ref, r, sem_ref.at[1]))

    for r in range(rows):
        c1, c2 = copies(r)
        _start_row_copy(c1, 0)
        _start_row_copy(c2, 1)
    pltpu.make_async_copy(ys_hbm.at[pl.ds(0, rows)], b1_ref, sem_ref.at[0]).wait()
    pltpu.make_async_copy(ys_hbm.at[pl.ds(0, rows)], b2_ref, sem_ref.at[1]).wait()

    rinfo = rinfo_ref[...]
    w1 = rinfo[:, R_W1:R_W1 + 1]
    w2 = rinfo[:, R_W2:R_W2 + 1]
    o_ref[...] = xmid_ref[...] + w1 * b1_ref[...] + w2 * b2_ref[...]


def _combine(pos3, xmid, rinfo, ys, *, tc):
    t, d = xmid.shape
    return pl.pallas_call(
        _combine_kernel,
        grid=(t // tc,),
        in_specs=[
            pl.BlockSpec((1, 1, 2 * tc), lambda i: (i, 0, 0), memory_space=pltpu.SMEM),
            pl.BlockSpec((tc, d), lambda i: (i, 0)),
            pl.BlockSpec((tc, LANES), lambda i: (i, 0)),
            pl.BlockSpec(memory_space=pl.ANY),
        ],
        out_specs=pl.BlockSpec((tc, d), lambda i: (i, 0)),
        out_shape=jax.ShapeDtypeStruct((t, d), F32),
        scratch_shapes=[pltpu.VMEM((tc, d), F32), pltpu.VMEM((tc, d), F32),
                        pltpu.SemaphoreType.DMA((2,))],
        compiler_params=_params(("arbitrary",)),
        name="moe_combine",
    )(pos3, xmid, rinfo, ys)


def _moe(x_mid, xs, rinfo, counts_row, dest, wg, wu, wd, layer):
    t = x_mid.shape[0]
    ne = wg.shape[1]
    capacity = t
    assert capacity % MOE_TILE == 0
    n_tiles = (2 * t) // MOE_TILE + ne
    dest = dest[:, 0:2, :]
    e_id = dest // capacity
    rank = dest - e_id * capacity
    counts = counts_row[0, N_GROUPS:N_GROUPS + ne].astype(jnp.int32)
    tiles_per = (counts + MOE_TILE - 1) // MOE_TILE
    tile_end = jnp.cumsum(tiles_per)
    tile_first = tile_end - tiles_per
    tile = jnp.arange(n_tiles, dtype=jnp.int32)
    tile_expert = jnp.minimum(
        jnp.sum((tile_end[None, :] <= tile[:, None]).astype(jnp.int32), axis=1), ne - 1)
    local = tile - tile_first[tile_expert]
    used = tile < tile_end[-1]
    tile_block = jnp.where(used, tile_expert * (capacity // MOE_TILE) + local, 0)
    tile_valid = jnp.where(used, jnp.clip(counts[tile_expert] - local * MOE_TILE, 0, MOE_TILE), 0)
    ys = _moe_experts(tile_block, tile_expert, tile_valid, xs, wg, wu, wd, layer)
    base = jnp.sum(jnp.where(e_id[..., None] == jnp.arange(ne, dtype=jnp.int32),
                             tile_first * MOE_TILE, 0), axis=-1)
    pos = base + rank
    steps, _, tc = dest.shape
    return _combine(pos.reshape(steps, 1, 2 * tc), x_mid, rinfo, ys, tc=tc)


def kernel(x, norm_mix_gain, w_in, a_q_gain, a_k_gain, rel_bias, b_gate_up, b_gate_bias,
           b_out_gain, proj_a, proj_b, w_out, norm_ffn_gain, w_group, b_group, w_router,
           b_router, w_gate, w_up, w_down):
    bsz, seq, d = x.shape
    depth = w_in.shape[0]
    t = bsz * seq
    a_width = proj_a.shape[1]
    a_hd = a_width // A_HEADS
    b_vw = proj_b.shape[1]
    b_qkw = b_gate_up.shape[2]
    dk, dv = b_qkw // B_HEADS, b_vw // B_HEADS
    assert seq % ATT_Q == 0 and seq % GLA_ROWS == 0 and d == 2 * a_width == 2 * b_vw
    assert b_qkw * 2 == a_width and N_GROUPS + N_EXPERTS <= LANES

    o_lr = 3 * a_width + 2 * b_qkw + 2 * b_vw
    qk_col = 3 * a_width // b_qkw
    v_col = (3 * a_width + 2 * b_qkw) // b_vw
    r_col = v_col + 1
    ga_col = (3 * a_width + 2 * b_qkw + 2 * b_vw) // d
    gb_col = ga_col + 1
    assert (3 * a_width + 2 * b_qkw + 2 * b_vw) % d == 0

    head_id = jnp.arange(a_width) // a_hd
    ones_bd = (head_id[:, None] == head_id[None, :]).astype(BF16)
    tri = jnp.tril(jnp.ones((CHUNK, CHUNK), BF16))
    bias_tile = _attn_bias_tile(rel_bias)

    x2d = x.reshape(t, d)
    tm = 512
    for l in range(depth):
        w_main = jnp.concatenate([w_in[l][:, :o_lr], w_in[l][:, o_lr + B_GATE_RANK:]],
                                 axis=1).astype(BF16)
        w_lr = jnp.pad(w_in[l][:, o_lr:o_lr + B_GATE_RANK],
                       ((0, 0), (0, LANES - B_GATE_RANK))).astype(BF16)
        u, lr = _inproj(x2d, norm_mix_gain[l][None, :], w_main, w_lr,
                        a_q_gain[l].reshape(1, a_width), a_k_gain[l].reshape(1, a_width),
                        ones_bd, tm=tm, a_width=a_width, head_dim=a_hd)
        u3 = u.reshape(bsz, seq, u.shape[1])
        y_a = _attention(u3, bias_tile, heads=A_HEADS, head_dim=a_hd)
        gup_pad = jnp.pad(b_gate_up[l], ((0, LANES - B_GATE_RANK), (0, 0)))
        y_b = _gla(u3, lr.reshape(bsz, seq, LANES), gup_pad, b_gate_bias[l][None, :],
                   b_out_gain[l].reshape(1, b_vw), tri, heads=B_HEADS, dk=dk, dv=dv,
                   qk_col=qk_col, v_col=v_col, r_col=r_col)
        w_r = jnp.pad(jnp.concatenate([w_group[l], w_router[l]], axis=1),
                      ((0, 0), (0, LANES - N_GROUPS - N_EXPERTS)))
        b_r = jnp.pad(jnp.concatenate([b_group[l], b_router[l]]),
                      (0, LANES - N_GROUPS - N_EXPERTS))[None, :]
        x_mid, rinfo, counts, dest, xs = _post(
            x2d, y_a.reshape(t, a_width), y_b.reshape(t, b_vw), u, proj_a[l].astype(BF16),
            proj_b[l].astype(BF16), w_out[l].astype(BF16), norm_ffn_gain[l][None, :], w_r, b_r,
            tm=tm, n_experts=N_EXPERTS, ga_col=ga_col, gb_col=gb_col)
        x2d = _moe(x_mid, xs, rinfo, counts, dest, w_gate, w_up, w_down, l)
    return x2d.reshape(bsz, seq, d)
```
